```python
import math
import jax, jax.numpy as jnp
from jax import lax
import numpy as np

D_MODEL = 1024
BATCH = 8
SEQ = 8192
DEPTH = 4
DEC_BATCH = 32
DEC_SEQ = 64
PAST_LEN = 2048

CHUNK = 64
HEAD_DIM = 64
A_HEADS = 4
A_BAND_CHUNKS = 8
A_REL_MAX = 128
A_REL_SIZE = (CHUNK - 1) + A_REL_MAX + 1
MLA_HEADS = 4
MLA_Q_RANK = 256
MLA_KV_RANK = 256
MLA_NOPE = 128
MLA_ROPE = 64
MLA_V = 128
SB_HEADS = 4
Q_BLOCK = 128
K_BLOCK = 128
Q_GROUPS = 8
BIG_POS = 2 ** 30
D_FF = 4 * D_MODEL
ROPE_THETA = 10000.0
EPS = 1e-6
NEG = -1e30
A_W = A_HEADS * HEAD_DIM
MLA_W = MLA_HEADS * MLA_V
SB_W = SB_HEADS * HEAD_DIM
MIX_W = A_W + MLA_W + SB_W
IN_COLS = 3 * A_W + MLA_Q_RANK + MLA_KV_RANK + MLA_ROPE + 3 * SB_W

kernel_name = "hybrid_chunkband_mla_stickbreak_stream_step"


def rmsnorm(x, g):
    xf = x.astype(jnp.float32)
    y = xf * lax.rsqrt(jnp.mean(xf * xf, axis=-1, keepdims=True) + EPS)
    return (y * g.astype(jnp.float32)).astype(x.dtype)


def rope(x, pos):
    half = x.shape[-1] // 2
    inv = ROPE_THETA ** (-jnp.arange(half, dtype=jnp.float32) / half)
    ang = pos.astype(jnp.float32)[:, None] * inv[None, :]
    shape = (1, pos.shape[0]) + (1,) * (x.ndim - 3) + (half,)
    cos, sin = jnp.cos(ang).reshape(shape), jnp.sin(ang).reshape(shape)
    x1, x2 = x[..., :half], x[..., half:]
    return jnp.concatenate([x1 * cos - x2 * sin, x1 * sin + x2 * cos], axis=-1).astype(x.dtype)


def _split_proj(h, pos, w_in, g_cq, g_ckv):
    B, S, _ = h.shape
    sizes = [A_W, A_W, A_W, MLA_Q_RANK, MLA_KV_RANK, MLA_ROPE, SB_W, SB_W, SB_W]
    cuts = [int(c) for c in np.cumsum(sizes)[:-1]]
    p = jnp.einsum('bsd,de->bse', h, w_in)
    qa, ka, va, cq, ckv, kr, qc, kc, vc = jnp.split(p, cuts, axis=-1)
    hd = lambda t: t.reshape(B, S, -1, HEAD_DIM)
    return (hd(qa), hd(ka), hd(va), rmsnorm(cq, g_cq), rmsnorm(ckv, g_ckv), rope(kr, pos),
            hd(qc), hd(kc), hd(vc))


def _rel_bias(rel, dist):
    idx = jnp.clip(dist, -(CHUNK - 1), A_REL_MAX) + (CHUNK - 1)
    return rel[:, idx].astype(jnp.float32)


def _band_prompt(q, k, v, rel):
    B, S, H, d = q.shape
    nc = S // CHUNK
    nb = A_BAND_CHUNKS + 1
    blk = lambda t: t.reshape(B, nc, CHUNK, H, d)

    def band(t):
        tp = jnp.pad(blk(t), ((0, 0), (A_BAND_CHUNKS, 0), (0, 0), (0, 0), (0, 0)))
        return jnp.concatenate([tp[:, i:i + nc] for i in range(nb)], axis=2)

    kb, vb = band(k), band(v)
    kpos = jnp.arange(nb * CHUNK)
    dist = A_BAND_CHUNKS * CHUNK + jnp.arange(CHUNK)[:, None] - kpos[None, :]
    bias = _rel_bias(rel, dist)
    valid = (jnp.arange(nc)[:, None] + kpos[None, :] // CHUNK) >= A_BAND_CHUNKS
    s = jnp.einsum('bcqhd,bckhd->bchqk', blk(q), kb).astype(jnp.float32) * HEAD_DIM ** -0.5 + bias[None, None]
    s = jnp.where(valid[None, :, None, None, :], s, NEG)
    p = jax.nn.softmax(s, axis=-1).astype(v.dtype)
    return jnp.einsum('bchqk,bckhd->bcqhd', p, vb).reshape(B, S, H, d)


def _band_sample(q, k, v, rel, n_cached):
    T, K = q.shape[1], k.shape[1]
    dist = (n_cached + jnp.arange(T))[:, None] - jnp.arange(K)[None, :]
    s = jnp.einsum('bqhd,bkhd->bhqk', q, k).astype(jnp.float32) * HEAD_DIM ** -0.5 + _rel_bias(rel, dist)[None]
    p = jax.nn.softmax(s, axis=-1).astype(v.dtype)
    return jnp.einsum('bhqk,bkhd->bqhd', p, v)


def _mla_q(cq, pos, w_uq):
    q = jnp.einsum('bsr,rhe->bshe', cq, w_uq)
    return q[..., :MLA_NOPE], rope(q[..., MLA_NOPE:], pos)


def _mla_kv(ckv, w_ukv):
    kv = jnp.einsum('bsr,rhe->bshe', ckv, w_ukv)
    return kv[..., :MLA_NOPE], kv[..., MLA_NOPE:]


def _mla_core(qn, qp, qpos, kn, kp, v, kpos):
    s = (jnp.einsum('bqhe,bkhe->bhqk', qn, kn).astype(jnp.float32)
         + jnp.einsum('bqhr,bkr->bhqk', qp, kp).astype(jnp.float32)) * (MLA_NOPE + MLA_ROPE) ** -0.5
    vis = (kpos[None, :] // CHUNK) <= (qpos[:, None] // CHUNK)
    s = jnp.where(vis, s, NEG)
    p = jax.nn.softmax(s, axis=-1).astype(v.dtype)
    return jnp.einsum('bhqk,bkhe->bqhe', p, v)


def _sb_core(q, qpos, k, v, kpos):
    B, K, H, d = k.shape
    pad = (-K) % K_BLOCK
    k = jnp.pad(k, ((0, 0), (0, pad), (0, 0), (0, 0)))
    v = jnp.pad(v, ((0, 0), (0, pad), (0, 0), (0, 0)))
    kpos = jnp.pad(kpos, (0, pad), constant_values=BIG_POS)
    nk = (K + pad) // K_BLOCK
    kb = k.reshape(B, nk, K_BLOCK, H, d)
    vb = v.reshape(B, nk, K_BLOCK, H, d)
    z = jnp.einsum('bqhd,bnjhd->bhqnj', q, kb).astype(jnp.float32) * HEAD_DIM ** -0.5
    causal = kpos.reshape(nk, K_BLOCK)[None] < qpos[:, None, None]
    l = jnp.where(causal, jax.nn.log_sigmoid(-z), 0.0)
    idx = jnp.arange(K_BLOCK)
    tri = (idx[:, None] >= idx[None, :]).astype(jnp.float32)
    inner = jnp.einsum('bhqnj,jk->bhqnk', l, tri)
    bidx = jnp.arange(nk)
    later = jnp.einsum('bhqm,mn->bhqn', inner[..., 0],
                       (bidx[:, None] > bidx[None, :]).astype(jnp.float32))
    a = jnp.where(causal, jnp.exp(jnp.minimum(z + inner + later[..., None], 0.0)), 0.0)
    return jnp.einsum('bhqnj,bnjhd->bqhd', a.astype(v.dtype), vb)


def _causal_sweep(core, qs, qpos, kvs, kpos):
    S = qpos.shape[0]
    nb = S // Q_BLOCK
    ng = min(Q_GROUPS, nb)
    bounds = [(i * nb) // ng for i in range(ng + 1)]
    outs = []
    for g0, g1 in zip(bounds[:-1], bounds[1:]):
        q0, q1, n = g0 * Q_BLOCK, g1 * Q_BLOCK, g1 - g0
        ks = tuple(t[:, :q1] for t in kvs)
        kp = kpos[:q1]
        blk = lambda t: jnp.moveaxis(t[:, q0:q1].reshape((t.shape[0], n, Q_BLOCK) + t.shape[2:]), 1, 0)
        xs = tuple(blk(t) for t in qs) + (qpos[q0:q1].reshape(n, Q_BLOCK),)
        out = lax.map(lambda a: core(*a[:-1], a[-1], *ks, kp), xs)
        out = jnp.moveaxis(out, 0, 1)
        outs.append(out.reshape((out.shape[0], n * Q_BLOCK) + out.shape[3:]))
    return jnp.concatenate(outs, axis=1)


def _merge(oa, om, osb, g_oa, g_om, g_os, w_out):
    B, S = oa.shape[:2]
    cat = jnp.concatenate([rmsnorm(oa.reshape(B, S, A_W), g_oa),
                           rmsnorm(om.reshape(B, S, MLA_W), g_om),
                           rmsnorm(osb.reshape(B, S, SB_W), g_os)], axis=-1)
    return jnp.einsum('bse,ed->bsd', cat, w_out)


def _ffn(h, w_up, w_down):
    u = jax.nn.relu(jnp.einsum('bsd,df->bsf', h, w_up))
    return jnp.einsum('bsf,fd->bsd', u * u, w_down)


def _mix_prompt(h, pos, w_in, g_cq, g_ckv, w_uq, w_ukv, a_rel, g_oa, g_om, g_os, w_out):
    S = h.shape[1]
    qa, ka, va, cq, ckv, kr, qc, kc, vc = _split_proj(h, pos, w_in, g_cq, g_ckv)
    oa = _band_prompt(qa, ka, va, a_rel)
    qn, qp = _mla_q(cq, pos, w_uq)
    kn, vm = _mla_kv(ckv, w_ukv)
    om = _causal_sweep(_mla_core, (qn, qp), pos, (kn, kr, vm), pos)
    osb = _causal_sweep(_sb_core, (qc,), pos, (kc, vc), pos)
    y = _merge(oa, om, osb, g_oa, g_om, g_os, w_out)
    lc = min(A_BAND_CHUNKS * CHUNK, S)
    return y, (ka[:, -lc:], va[:, -lc:], ckv, kr, kc, vc)


def _mix_sample(h, c_ak, c_av, c_ckv, c_kr, c_sk, c_sv,
                w_in, g_cq, g_ckv, w_uq, w_ukv, a_rel, g_oa, g_om, g_os, w_out):
    T = h.shape[1]
    n_past = c_ckv.shape[1]
    n_band = c_ak.shape[1]
    pos = n_past + jnp.arange(T)
    kpos = jnp.arange(n_past + T)
    qa, ka, va, cq, ckv, kr, qc, kc, vc = _split_proj(h, pos, w_in, g_cq, g_ckv)
    k_band = jnp.concatenate([c_ak, ka], axis=1)
    v_band = jnp.concatenate([c_av, va], axis=1)
    oa = _band_sample(qa, k_band, v_band, a_rel, n_band)
    qn, qp = _mla_q(cq, pos, w_uq)
    kn, vm = _mla_kv(jnp.concatenate([c_ckv, ckv], axis=1), w_ukv)
    om = _mla_core(qn, qp, pos, kn, jnp.concatenate([c_kr, kr], axis=1), vm, kpos)
    osb = _sb_core(qc, pos, jnp.concatenate([c_sk, kc], axis=1), jnp.concatenate([c_sv, vc], axis=1), kpos)
    y = _merge(oa, om, osb, g_oa, g_om, g_os, w_out)
    return y, (k_band[:, -n_band:], v_band[:, -n_band:], ckv, kr, kc, vc)


def setup_inputs(seed: int = 0) -> dict:
    key = jax.random.key(seed)
    ks = jax.random.split(key, 26)
    f32 = jnp.float32
    nrm = lambda k, shape, scale: jax.random.normal(k, shape, f32) * scale
    gain = lambda k, shape: 1.0 + 0.02 * jax.random.normal(k, shape, f32)
    la = min(A_BAND_CHUNKS * CHUNK, PAST_LEN)
    return {
        "x_prompt": nrm(ks[0], (BATCH, SEQ, D_MODEL), 1.0),
        "x_sample": nrm(ks[1], (DEC_BATCH, DEC_SEQ, D_MODEL), 1.0),
        "cache_a_k": nrm(ks[2], (DEPTH, DEC_BATCH, la, A_HEADS, HEAD_DIM), 1.0),
        "cache_a_v": nrm(ks[3], (DEPTH, DEC_BATCH, la, A_HEADS, HEAD_DIM), 1.0),
        "cache_mla_ckv": nrm(ks[4], (DEPTH, DEC_BATCH, PAST_LEN, MLA_KV_RANK), 1.0),
        "cache_mla_krope": nrm(ks[5], (DEPTH, DEC_BATCH, PAST_LEN, MLA_ROPE), 1.0),
        "cache_sb_k": nrm(ks[6], (DEPTH, DEC_BATCH, PAST_LEN, SB_HEADS, HEAD_DIM), 1.0),
        "cache_sb_v": nrm(ks[7], (DEPTH, DEC_BATCH, PAST_LEN, SB_HEADS, HEAD_DIM), 1.0),
        "g_mix": gain(ks[8], (DEPTH, D_MODEL)),
        "w_in": nrm(ks[9], (DEPTH, D_MODEL, IN_COLS), D_MODEL ** -0.5),
        "g_cq": gain(ks[10], (DEPTH, MLA_Q_RANK)),
        "g_ckv": gain(ks[11], (DEPTH, MLA_KV_RANK)),
        "w_uq": nrm(ks[12], (DEPTH, MLA_Q_RANK, MLA_HEADS, MLA_NOPE + MLA_ROPE), MLA_Q_RANK ** -0.5),
        "w_ukv": nrm(ks[13], (DEPTH, MLA_KV_RANK, MLA_HEADS, MLA_NOPE + MLA_V), MLA_KV_RANK ** -0.5),
        "a_rel_bias": nrm(ks[14], (DEPTH, A_HEADS, A_REL_SIZE), 0.5),
        "g_out_a": gain(ks[15], (DEPTH, A_W)),
        "g_out_mla": gain(ks[16], (DEPTH, MLA_W)),
        "g_out_sb": gain(ks[17], (DEPTH, SB_W)),
        "w_out": nrm(ks[18], (DEPTH, MIX_W, D_MODEL), (2.0 * MIX_W) ** -0.5),
        "g_ffn": gain(ks[19], (DEPTH, D_MODEL)),
        "w_up": nrm(ks[20], (DEPTH, D_MODEL, D_FF), D_MODEL ** -0.5),
        "w_down": nrm(ks[21], (DEPTH, D_FF, D_MODEL), (2.0 * D_FF) ** -0.5),
        "g_final": gain(ks[22], (D_MODEL,)),
    }


def reference(x_prompt, x_sample, cache_a_k, cache_a_v, cache_mla_ckv, cache_mla_krope, cache_sb_k, cache_sb_v,
              g_mix, w_in, g_cq, g_ckv, w_uq, w_ukv, a_rel_bias, g_out_a, g_out_mla, g_out_sb, w_out,
              g_ffn, w_up, w_down, g_final):
    xp, xs = x_prompt, x_sample
    pos_p = jnp.arange(xp.shape[1])
    prompt_states, sample_states = [], []
    for l in range(DEPTH):
        lw = (w_in[l], g_cq[l], g_ckv[l], w_uq[l], w_ukv[l], a_rel_bias[l],
              g_out_a[l], g_out_mla[l], g_out_sb[l], w_out[l])
        yp, st_p = _mix_prompt(rmsnorm(xp, g_mix[l]), pos_p, *lw)
        ys, st_s = _mix_sample(rmsnorm(xs, g_mix[l]), cache_a_k[l], cache_a_v[l], cache_mla_ckv[l],
                               cache_mla_krope[l], cache_sb_k[l], cache_sb_v[l], *lw)
        xp = xp + yp
        xs = xs + ys
        xp = xp + _ffn(rmsnorm(xp, g_ffn[l]), w_up[l], w_down[l])
        xs = xs + _ffn(rmsnorm(xs, g_ffn[l]), w_up[l], w_down[l])
        prompt_states.append(st_p)
        sample_states.append(st_s)
    p_a_k, p_a_v, p_ckv, p_krope, p_sb_k, p_sb_v = [jnp.stack(t, axis=0) for t in zip(*prompt_states)]
    s_a_k, s_a_v, s_ckv, s_krope, s_sb_k, s_sb_v = [jnp.stack(t, axis=0) for t in zip(*sample_states)]
    y_prompt = rmsnorm(xp, g_final)
    y_sample = rmsnorm(xs, g_final)
    return (y_prompt, y_sample, p_a_k, p_a_v, p_ckv, p_krope, p_sb_k, p_sb_v,
            s_a_k, s_a_v, s_ckv, s_krope, s_sb_k, s_sb_v)
```

```python
import functools
import math

import numpy as np
import jax
import jax.numpy as jnp
from jax import lax
from jax.experimental import pallas as pl
from jax.experimental.pallas import tpu as pltpu

CHUNK = 64
HEAD_DIM = 64
N_HEADS = 4
BAND = 8 * CHUNK
REL_MAX = 128
MLA_NOPE = 128
MLA_ROPE = 64
MLA_V = 128
MLA_RANK = 256
ROPE_THETA = 10000.0
EPS = 1e-6
NEG = -1e30

HEADS_W = N_HEADS * HEAD_DIM
MLA_QK = 256
ROPE_PAD = 128

TQ_BAND = 256
T_MLA = 256
T_SB = 256
TM_IN = 512
TM_FFN = 512
TF_FFN = 1024
VMEM_LIMIT = 56 * 1024 * 1024

_DN_T = (((1,), (1,)), ((), ()))


def _params(sem):
    return pltpu.CompilerParams(dimension_semantics=sem, vmem_limit_bytes=VMEM_LIMIT)


def _rms(x, g):
    return x * lax.rsqrt(jnp.mean(x * x, axis=-1, keepdims=True) + EPS) * g


def _head_mask(width, h):
    lane = lax.broadcasted_iota(jnp.int32, (1, width), 1)
    return (lane // HEAD_DIM) == h


def _in_proj_kernel(x_ref, g_ref, w_ref, gcq_ref, gckv_ref, wuq_ref, wukv_ref, cos_ref, sin_ref,
                    qa_ref, ka_ref, va_ref, kaf_ref, vaf_ref, ckv_ref, kr_ref,
                    qc_ref, kc_ref, vc_ref, kcf_ref, vcf_ref, qm_ref, km_ref, vm_ref):
    bf = jnp.bfloat16
    h = _rms(x_ref[...], g_ref[...]).astype(bf)
    p = jnp.dot(h, w_ref[...], preferred_element_type=jnp.float32)
    W = HEADS_W
    qa_ref[...] = (p[:, 0:W] * HEAD_DIM ** -0.5).astype(bf)
    ka = p[:, W:2 * W]
    va = p[:, 2 * W:3 * W]
    ka_ref[...] = ka.astype(bf)
    va_ref[...] = va.astype(bf)
    kaf_ref[...] = ka
    vaf_ref[...] = va
    cq = _rms(p[:, 3 * W:4 * W], gcq_ref[...])
    ckv = _rms(p[:, 4 * W:5 * W], gckv_ref[...])
    ckv_ref[...] = ckv
    qc_ref[...] = (p[:, 5 * W:6 * W] * HEAD_DIM ** -0.5).astype(bf)
    kc = p[:, 6 * W:7 * W]
    vc = p[:, 7 * W:8 * W]
    kc_ref[...] = kc.astype(bf)
    vc_ref[...] = vc.astype(bf)
    kcf_ref[...] = kc
    vcf_ref[...] = vc
    cos = cos_ref[...]
    sin = sin_ref[...]
    kr = p[:, 8 * W:8 * W + ROPE_PAD] * cos + p[:, 8 * W + ROPE_PAD:8 * W + 2 * ROPE_PAD] * sin
    kr_ref[...] = kr[:, :MLA_ROPE]
    krb = kr.astype(bf)
    q = jnp.dot(cq.astype(bf), wuq_ref[...], preferred_element_type=jnp.float32)
    kv = jnp.dot(ckv.astype(bf), wukv_ref[...], preferred_element_type=jnp.float32)
    scale = (MLA_NOPE + MLA_ROPE) ** -0.5
    nq = N_HEADS * MLA_NOPE
    for hh in range(N_HEADS):
        qn = q[:, hh * MLA_NOPE:(hh + 1) * MLA_NOPE]
        qp = q[:, nq + hh * ROPE_PAD:nq + (hh + 1) * ROPE_PAD]
        qs = q[:, nq + (N_HEADS + hh) * ROPE_PAD:nq + (N_HEADS + hh + 1) * ROPE_PAD]
        qm_ref[hh, :, 0:MLA_NOPE] = (qn * scale).astype(bf)
        qm_ref[hh, :, MLA_NOPE:MLA_QK] = ((qp * cos + qs * sin) * scale).astype(bf)
        km_ref[hh, :, 0:MLA_NOPE] = kv[:, hh * MLA_NOPE:(hh + 1) * MLA_NOPE].astype(bf)
        km_ref[hh, :, MLA_NOPE:MLA_QK] = krb
    vm_ref[...] = kv[:, nq:].astype(bf)


def _in_proj(x, g, w_in, g_cq, g_ckv, w_uq, w_ukv, cos, sin, rows_per_seq, tail_rows):
    N, D = x.shape
    tm = min(TM_IN, N)
    assert N % tm == 0 and cos.shape[0] % tm == 0
    n_tab = cos.shape[0] // tm
    n_seq = N // rows_per_seq
    if tail_rows == rows_per_seq:
        tail_map = lambda i: (i, 0)
    else:
        assert tail_rows % tm == 0 and rows_per_seq % tm == 0
        tiles_per_seq = rows_per_seq // tm
        tail_tiles = tail_rows // tm

        def tail_map(i):
            b = i // tiles_per_seq
            t = i % tiles_per_seq
            return (b * tail_tiles + jnp.maximum(t - (tiles_per_seq - tail_tiles), 0), 0)

    row = lambda w: pl.BlockSpec((tm, w), lambda i: (i, 0))
    full = lambda a: pl.BlockSpec(a.shape, lambda i: (0,) * a.ndim)
    tab = pl.BlockSpec((tm, ROPE_PAD), lambda i: (i % n_tab, 0))
    tail = pl.BlockSpec((tm, HEADS_W), tail_map)
    heads = pl.BlockSpec((N_HEADS, tm, MLA_QK), lambda i: (0, i, 0))
    bf, f32 = jnp.bfloat16, jnp.float32
    sds = jax.ShapeDtypeStruct
    out_shape = [
        sds((N, HEADS_W), bf), sds((N, HEADS_W), bf), sds((N, HEADS_W), bf),
        sds((n_seq * tail_rows, HEADS_W), f32), sds((n_seq * tail_rows, HEADS_W), f32),
        sds((N, MLA_RANK), f32), sds((N, MLA_ROPE), f32),
        sds((N, HEADS_W), bf), sds((N, HEADS_W), bf), sds((N, HEADS_W), bf),
        sds((N, HEADS_W), f32), sds((N, HEADS_W), f32),
        sds((N_HEADS, N, MLA_QK), bf), sds((N_HEADS, N, MLA_QK), bf),
        sds((N, N_HEADS * MLA_V), bf),
    ]
    out_specs = [row(HEADS_W), row(HEADS_W), row(HEADS_W), tail, tail,
                 row(MLA_RANK), row(MLA_ROPE),
                 row(HEADS_W), row(HEADS_W), row(HEADS_W), row(HEADS_W), row(HEADS_W),
                 heads, heads, row(N_HEADS * MLA_V)]
    return pl.pallas_call(
        _in_proj_kernel,
        grid=(N // tm,),
        in_specs=[row(D), full(g), full(w_in), full(g_cq), full(g_ckv), full(w_uq), full(w_ukv), tab, tab],
        out_specs=out_specs,
        out_shape=out_shape,
        compiler_params=_params(("arbitrary",)),
        name="in_proj",
    )(x, g, w_in, g_cq, g_ckv, w_uq, w_ukv, cos, sin)


def _band_heads(q, kparts, vparts, biases, valid):
    out = jnp.zeros((q.shape[0], HEADS_W), jnp.float32)
    for h in range(N_HEADS):
        hm = _head_mask(HEADS_W, h)
        qh = jnp.where(hm, q, jnp.zeros_like(q))
        ss = []
        for i, k in enumerate(kparts):
            s = lax.dot_general(qh, k, _DN_T, preferred_element_type=jnp.float32) + biases[i][h]
            if valid is not None and valid[i] is not None:
                s = jnp.where(valid[i], s, NEG)
            ss.append(s)
        m = functools.reduce(jnp.maximum, [jnp.max(s, axis=-1, keepdims=True) for s in ss])
        ps = [jnp.exp(s - m) for s in ss]
        den = functools.reduce(jnp.add, [jnp.sum(p, axis=-1, keepdims=True) for p in ps])
        o = functools.reduce(jnp.add, [jnp.dot(p.astype(jnp.bfloat16), v, preferred_element_type=jnp.float32)
                                       for p, v in zip(ps, vparts)])
        out = out + jnp.where(hm, o / den, 0.0)
    return out


def _band_prompt_kernel(q_ref, k_ref, v_ref, bias_ref, o_ref):
    i = pl.program_id(1)
    T = TQ_BAND
    nprev = BAND // T
    starts = [pl.multiple_of(jnp.maximum(i - (nprev - j), 0) * T, T) for j in range(nprev + 1)]
    k = jnp.concatenate([k_ref[pl.ds(s, T), :] for s in starts], axis=0)
    v = jnp.concatenate([v_ref[pl.ds(s, T), :] for s in starts], axis=0)
    col = lax.broadcasted_iota(jnp.int32, (1, BAND + T), 1)
    valid = col >= BAND - i * T
    o_ref[...] = _band_heads(q_ref[...], [k], [v], [bias_ref], [valid])


def _band_prompt(q, k, v, bias):
    B, S, W = q.shape
    T = TQ_BAND
    assert S % T == 0 and BAND % T == 0
    return pl.pallas_call(
        _band_prompt_kernel,
        grid=(B, S // T),
        in_specs=[pl.BlockSpec((None, T, W), lambda b, i: (b, i, 0)),
                  pl.BlockSpec((None, S, W), lambda b, i: (b, 0, 0)),
                  pl.BlockSpec((None, S, W), lambda b, i: (b, 0, 0)),
                  pl.BlockSpec(bias.shape, lambda b, i: (0, 0, 0))],
        out_specs=pl.BlockSpec((None, T, W), lambda b, i: (b, i, 0)),
        out_shape=jax.ShapeDtypeStruct((B, S, W), jnp.float32),
        compiler_params=_params(("arbitrary", "arbitrary")),
        name="band_prompt",
    )(q, k, v, bias)


def _band_sample_kernel(q_ref, kn_ref, vn_ref, ck_ref, cv_ref, bc_ref, bn_ref, o_ref, sk_ref, sv_ref):
    bf = jnp.bfloat16
    ck, cv, kn, vn = ck_ref[...], cv_ref[...], kn_ref[...], vn_ref[...]
    o_ref[...] = _band_heads(q_ref[...], [ck.astype(bf), kn.astype(bf)], [cv.astype(bf), vn.astype(bf)],
                             [bc_ref, bn_ref], None)
    n_keep = ck.shape[0] - kn.shape[0]
    sk_ref[0:n_keep, :] = ck[kn.shape[0]:, :]
    sk_ref[n_keep:, :] = kn
    sv_ref[0:n_keep, :] = cv[vn.shape[0]:, :]
    sv_ref[n_keep:, :] = vn


def _band_sample(q, k_new, v_new, cache_k, cache_v, layer, bias_c, bias_n):
    B, T, W = q.shape
    LA = cache_k.shape[2]
    new = pl.BlockSpec((None, T, W), lambda b: (b, 0, 0))
    cache = pl.BlockSpec((None, None, LA, W), lambda b: (layer, b, 0, 0))
    roll = pl.BlockSpec((None, LA, W), lambda b: (b, 0, 0))
    full = lambda a: pl.BlockSpec(a.shape, lambda b: (0,) * a.ndim)
    return pl.pallas_call(
        _band_sample_kernel,
        grid=(B,),
        in_specs=[new, new, new, cache, cache, full(bias_c), full(bias_n)],
        out_specs=[new, roll, roll],
        out_shape=[jax.ShapeDtypeStruct((B, T, W), jnp.float32),
                   jax.ShapeDtypeStruct((B, LA, W), jnp.float32),
                   jax.ShapeDtypeStruct((B, LA, W), jnp.float32)],
        compiler_params=_params(("arbitrary",)),
        name="band_sample",
    )(q, k_new, v_new, cache_k, cache_v, bias_c, bias_n)


def _mla_prompt_kernel(q_ref, k_ref, v_ref, o_ref, m_ref, l_ref, acc_ref):
    i = pl.program_id(2)
    T = T_MLA
    q = q_ref[...]
    m_ref[...] = jnp.full(m_ref.shape, NEG, jnp.float32)
    l_ref[...] = jnp.zeros(l_ref.shape, jnp.float32)
    acc_ref[...] = jnp.zeros(acc_ref.shape, jnp.float32)

    def block(j, mask):
        start = pl.multiple_of(j * T, T)
        k = k_ref[pl.ds(start, T), :]
        v = v_ref[pl.ds(start, T), :]
        s = lax.dot_general(q, k, _DN_T, preferred_element_type=jnp.float32)
        if mask is not None:
            s = jnp.where(mask, s, NEG)
        m_old = m_ref[...]
        m_new = jnp.maximum(m_old, jnp.max(s, axis=-1, keepdims=True))
        alpha = jnp.exp(m_old - m_new)
        p = jnp.exp(s - m_new)
        l_ref[...] = alpha * l_ref[...] + jnp.sum(p, axis=-1, keepdims=True)
        acc_ref[...] = alpha * acc_ref[...] + jnp.dot(p.astype(jnp.bfloat16), v,
                                                      preferred_element_type=jnp.float32)
        m_ref[...] = m_new

    def body(j, c):
        block(j, None)
        return c

    lax.fori_loop(0, i, body, 0)
    r = lax.broadcasted_iota(jnp.int32, (T, T), 0) // CHUNK
    c = lax.broadcasted_iota(jnp.int32, (T, T), 1) // CHUNK
    block(i, c <= r)
    o_ref[...] = acc_ref[...] / l_ref[...]


def _mla_prompt(qm, km, vm):
    H, B, S, E = qm.shape
    T = T_MLA
    assert S % T == 0
    return pl.pallas_call(
        _mla_prompt_kernel,
        grid=(B, H, S // T),
        in_specs=[pl.BlockSpec((None, None, T, E), lambda b, h, i: (h, b, i, 0)),
                  pl.BlockSpec((None, None, S, E), lambda b, h, i: (h, b, 0, 0)),
                  pl.BlockSpec((None, S, MLA_V), lambda b, h, i: (b, 0, h))],
        out_specs=pl.BlockSpec((None, T, MLA_V), lambda b, h, i: (b, i, h)),
        out_shape=jax.ShapeDtypeStruct((B, S, H * MLA_V), jnp.float32),
        scratch_shapes=[pltpu.VMEM((T, 1), jnp.float32), pltpu.VMEM((T, 1), jnp.float32),
                        pltpu.VMEM((T, MLA_V), jnp.float32)],
        compiler_params=_params(("arbitrary", "arbitrary", "arbitrary")),
        name="mla_prompt",
    )(qm, km, vm)


def _mla_sample_kernel(q_ref, kn_ref, vn_ref, ckv_ref, ckr_ref, wukv_ref, o_ref, kv_ref):
    bf = jnp.bfloat16
    P = ckv_ref.shape[0]
    step = min(512, P)
    for r in range(0, P, step):
        kv_ref[r:r + step, :] = jnp.dot(ckv_ref[r:r + step, :].astype(bf), wukv_ref[...],
                                        preferred_element_type=jnp.float32).astype(bf)
    ckr = ckr_ref[...].astype(bf)
    nk = N_HEADS * MLA_NOPE
    for h in range(N_HEADS):
        q = q_ref[h]
        s_c = (lax.dot_general(q[:, :MLA_NOPE], kv_ref[:, h * MLA_NOPE:(h + 1) * MLA_NOPE], _DN_T,
                               preferred_element_type=jnp.float32)
               + lax.dot_general(q[:, MLA_NOPE:MLA_NOPE + MLA_ROPE], ckr, _DN_T,
                                 preferred_element_type=jnp.float32))
        s_n = lax.dot_general(q, kn_ref[h], _DN_T, preferred_element_type=jnp.float32)
        m = jnp.maximum(jnp.max(s_c, axis=-1, keepdims=True), jnp.max(s_n, axis=-1, keepdims=True))
        p_c = jnp.exp(s_c - m)
        p_n = jnp.exp(s_n - m)
        den = jnp.sum(p_c, axis=-1, keepdims=True) + jnp.sum(p_n, axis=-1, keepdims=True)
        o = (jnp.dot(p_c.astype(bf), kv_ref[:, nk + h * MLA_V:nk + (h + 1) * MLA_V],
                     preferred_element_type=jnp.float32)
             + jnp.dot(p_n.astype(bf), vn_ref[:, h * MLA_V:(h + 1) * MLA_V],
                       preferred_element_type=jnp.float32))
        o_ref[:, h * MLA_V:(h + 1) * MLA_V] = o / den


def _mla_sample(qm, km, vm, cache_ckv, cache_kr, layer, w_ukv):
    H, B, T, E = qm.shape
    P = cache_ckv.shape[2]
    heads = pl.BlockSpec((H, None, T, E), lambda b: (0, b, 0, 0))
    return pl.pallas_call(
        _mla_sample_kernel,
        grid=(B,),
        in_specs=[heads, heads,
                  pl.BlockSpec((None, T, H * MLA_V), lambda b: (b, 0, 0)),
                  pl.BlockSpec((None, None, P, MLA_RANK), lambda b: (layer, b, 0, 0)),
                  pl.BlockSpec((None, None, P, MLA_ROPE), lambda b: (layer, b, 0, 0)),
                  pl.BlockSpec(w_ukv.shape, lambda b: (0, 0))],
        out_specs=pl.BlockSpec((None, T, H * MLA_V), lambda b: (b, 0, 0)),
        out_shape=jax.ShapeDtypeStruct((B, T, H * MLA_V), jnp.float32),
        scratch_shapes=[pltpu.VMEM((P, w_ukv.shape[1]), jnp.bfloat16)],
        compiler_params=_params(("arbitrary",)),
        name="mla_sample",
    )(qm, km, vm, cache_ckv, cache_kr, w_ukv)


def _tri(n):
    r = lax.broadcasted_iota(jnp.int32, (n, n), 0)
    c = lax.broadcasted_iota(jnp.int32, (n, n), 1)
    return jnp.where(r >= c, 1.0, 0.0).astype(jnp.bfloat16)


def _sb_block(qh, k, v, carry, tri, mask):
    z = lax.dot_general(qh, k, _DN_T, preferred_element_type=jnp.float32)
    lsg = -(jnp.maximum(z, 0.0) + jnp.log1p(jnp.exp(-jnp.abs(z))))
    if mask is not None:
        lsg = jnp.where(mask, lsg, 0.0)
    inner = jnp.dot(lsg.astype(jnp.bfloat16), tri, preferred_element_type=jnp.float32)
    a = jnp.exp(jnp.minimum(z + inner + carry, 0.0))
    if mask is not None:
        a = jnp.where(mask, a, 0.0)
    pv = jnp.dot(a.astype(jnp.bfloat16), v, preferred_element_type=jnp.float32)
    return pv, carry + inner[:, 0:1]


def _sb_prompt_kernel(q_ref, k_ref, v_ref, o_ref):
    i = pl.program_id(1)
    T = T_SB
    q = q_ref[...]
    tri = _tri(T)
    r = lax.broadcasted_iota(jnp.int32, (T, T), 0)
    c = lax.broadcasted_iota(jnp.int32, (T, T), 1)
    causal = c < r

    def head(h, out):
        hm = _head_mask(HEADS_W, h)
        qh = jnp.where(hm, q, jnp.zeros_like(q))
        d0 = pl.multiple_of(i * T, T)
        acc, carry = _sb_block(qh, k_ref[pl.ds(d0, T), :], v_ref[pl.ds(d0, T), :],
                               jnp.zeros((T, 1), jnp.float32), tri, causal)

        def body(n, st):
            acc, carry = st
            start = pl.multiple_of((i - 1 - n) * T, T)
            pv, carry = _sb_block(qh, k_ref[pl.ds(start, T), :], v_ref[pl.ds(start, T), :], carry, tri, None)
            return acc + pv, carry

        acc, _ = lax.fori_loop(0, i, body, (acc, carry))
        return out + jnp.where(hm, acc, 0.0)

    o_ref[...] = lax.fori_loop(0, N_HEADS, head, jnp.zeros((T, HEADS_W), jnp.float32))


def _sb_prompt(q, k, v):
    B, S, W = q.shape
    T = T_SB
    assert S % T == 0
    return pl.pallas_call(
        _sb_prompt_kernel,
        grid=(B, S // T),
        in_specs=[pl.BlockSpec((None, T, W), lambda b, i: (b, i, 0)),
                  pl.BlockSpec((None, S, W), lambda b, i: (b, 0, 0)),
                  pl.BlockSpec((None, S, W), lambda b, i: (b, 0, 0))],
        out_specs=pl.BlockSpec((None, T, W), lambda b, i: (b, i, 0)),
        out_shape=jax.ShapeDtypeStruct((B, S, W), jnp.float32),
        compiler_params=_params(("arbitrary", "arbitrary")),
        name="sb_prompt",
    )(q, k, v)


def _sb_sample_kernel(q_ref, kn_ref, vn_ref, ck_ref, cv_ref, o_ref, kb_ref, vb_ref):
    bf = jnp.bfloat16
    T = q_ref.shape[0]
    P = ck_ref.shape[0]
    TK = min(T_SB, P)
    kb_ref[...] = ck_ref[...].astype(bf)
    vb_ref[...] = cv_ref[...].astype(bf)
    q = q_ref[...]
    kn, vn = kn_ref[...], vn_ref[...]
    tri_n, tri_c = _tri(T), _tri(TK)
    r = lax.broadcasted_iota(jnp.int32, (T, T), 0)
    c = lax.broadcasted_iota(jnp.int32, (T, T), 1)
    causal = c < r

    def head(h, out):
        hm = _head_mask(HEADS_W, h)
        qh = jnp.where(hm, q, jnp.zeros_like(q))
        acc, carry = _sb_block(qh, kn, vn, jnp.zeros((T, 1), jnp.float32), tri_n, causal)
        for start in range(P - TK, -1, -TK):
            pv, carry = _sb_block(qh, kb_ref[start:start + TK, :], vb_ref[start:start + TK, :], carry, tri_c, None)
            acc = acc + pv
        return out + jnp.where(hm, acc, 0.0)

    o_ref[...] = lax.fori_loop(0, N_HEADS, head, jnp.zeros((T, HEADS_W), jnp.float32))


def _sb_sample(q, k_new, v_new, cache_k, cache_v, layer):
    B, T, W = q.shape
    P = cache_k.shape[2]
    assert P % min(T_SB, P) == 0
    new = pl.BlockSpec((None, T, W), lambda b: (b, 0, 0))
    cache = pl.BlockSpec((None, None, P, W), lambda b: (layer, b, 0, 0))
    return pl.pallas_call(
        _sb_sample_kernel,
        grid=(B,),
        in_specs=[new, new, new, cache, cache],
        out_specs=new,
        out_shape=jax.ShapeDtypeStruct((B, T, W), jnp.float32),
        scratch_shapes=[pltpu.VMEM((P, W), jnp.bfloat16), pltpu.VMEM((P, W), jnp.bfloat16)],
        compiler_params=_params(("arbitrary",)),
        name="sb_sample",
    )(q, k_new, v_new, cache_k, cache_v)


def _merge_ffn_kernel(final, x_ref, oa_ref, om_ref, os_ref, goa_ref, gom_ref, gos_ref, wout_ref,
                      gffn_ref, wup_ref, wdown_ref, gfin_ref, y_ref, x1_ref, h_ref, acc_ref):
    bf = jnp.bfloat16
    j = pl.program_id(1)

    @pl.when(j == 0)
    def _():
        cat = jnp.concatenate([_rms(oa_ref[...], goa_ref[...]).astype(bf),
                               _rms(om_ref[...], gom_ref[...]).astype(bf),
                               _rms(os_ref[...], gos_ref[...]).astype(bf)], axis=-1)
        x1 = x_ref[...] + jnp.dot(cat, wout_ref[...], preferred_element_type=jnp.float32)
        x1_ref[...] = x1
        h_ref[...] = _rms(x1, gffn_ref[...]).astype(bf)
        acc_ref[...] = jnp.zeros(acc_ref.shape, jnp.float32)

    u = jnp.maximum(jnp.dot(h_ref[...], wup_ref[...], preferred_element_type=jnp.float32), 0.0)
    acc_ref[...] += jnp.dot((u * u).astype(bf), wdown_ref[...], preferred_element_type=jnp.float32)

    @pl.when(j == pl.num_programs(1) - 1)
    def _():
        y = x1_ref[...] + acc_ref[...]
        y_ref[...] = _rms(y, gfin_ref[...]) if final else y


def _merge_ffn(x, oa, om, osb, g_oa, g_om, g_os, w_out, g_ffn, w_up, w_down, g_final, final):
    N, D = x.shape
    F = w_up.shape[1]
    tm = min(TM_FFN, N)
    tf = min(TF_FFN, F)
    assert N % tm == 0 and F % tf == 0
    row = lambda w: pl.BlockSpec((tm, w), lambda i, j: (i, 0))
    full = lambda a: pl.BlockSpec(a.shape, lambda i, j: (0,) * a.ndim)
    return pl.pallas_call(
        functools.partial(_merge_ffn_kernel, final),
        grid=(N // tm, F // tf),
        in_specs=[row(D), row(oa.shape[1]), row(om.shape[1]), row(osb.shape[1]),
                  full(g_oa), full(g_om), full(g_os), full(w_out), full(g_ffn),
                  pl.BlockSpec((D, tf), lambda i, j: (0, j)),
                  pl.BlockSpec((tf, D), lambda i, j: (j, 0)),
                  full(g_final)],
        out_specs=row(D),
        out_shape=jax.ShapeDtypeStruct((N, D), jnp.float32),
        scratch_shapes=[pltpu.VMEM((tm, D), jnp.float32), pltpu.VMEM((tm, D), jnp.bfloat16),
                        pltpu.VMEM((tm, D), jnp.float32)],
        compiler_params=_params(("arbitrary", "arbitrary")),
        name="merge_ffn",
    )(x, oa, om, osb, g_oa, g_om, g_os, w_out, g_ffn, w_up, w_down, g_final)


def _swap_halves(w):
    half = w.shape[-1] // 2
    return jnp.concatenate([w[..., half:], w[..., :half]], axis=-1)


def _pad_cols(w, width):
    return jnp.pad(w, [(0, 0)] * (w.ndim - 1) + [(0, width - w.shape[-1])])


def _prep_layer(w_in, w_uq, w_ukv):
    bf = jnp.bfloat16
    W = HEADS_W
    a_end = 3 * W
    cq_end = a_end + MLA_RANK
    ckv_end = cq_end + MLA_RANK
    kr_end = ckv_end + MLA_ROPE
    w_kr = w_in[:, ckv_end:kr_end]
    w_in_p = jnp.concatenate([w_in[:, :ckv_end], w_in[:, kr_end:],
                              _pad_cols(w_kr, ROPE_PAD), _pad_cols(_swap_halves(w_kr), ROPE_PAD)], axis=1)
    qn = w_uq[:, :, :MLA_NOPE].reshape(MLA_RANK, N_HEADS * MLA_NOPE)
    qp = w_uq[:, :, MLA_NOPE:]
    w_uq_p = jnp.concatenate([qn, _pad_cols(qp, ROPE_PAD).reshape(MLA_RANK, -1),
                              _pad_cols(_swap_halves(qp), ROPE_PAD).reshape(MLA_RANK, -1)], axis=1)
    w_ukv_p = jnp.concatenate([w_ukv[:, :, :MLA_NOPE].reshape(MLA_RANK, -1),
                               w_ukv[:, :, MLA_NOPE:].reshape(MLA_RANK, -1)], axis=1)
    return w_in_p.astype(bf), w_uq_p.astype(bf), w_ukv_p.astype(bf)


def _rope_tables(pos):
    half = MLA_ROPE // 2
    inv = ROPE_THETA ** (-jnp.arange(half, dtype=jnp.float32) / half)
    ang = pos.astype(jnp.float32)[:, None] * inv[None, :]
    cos, sin = jnp.cos(ang), jnp.sin(ang)
    return (_pad_cols(jnp.concatenate([cos, cos], axis=-1), ROPE_PAD),
            _pad_cols(jnp.concatenate([-sin, sin], axis=-1), ROPE_PAD))


def _band_bias_prompt(rel):
    T = TQ_BAND
    r = np.arange(T)[:, None]
    c = np.arange(BAND + T)[None, :]
    idx = np.clip(BAND + r - c, -(CHUNK - 1), REL_MAX) + (CHUNK - 1)
    dc = c // CHUNK - r // CHUNK
    vis = (dc >= 0) & (dc <= BAND // CHUNK)
    return jnp.where(jnp.asarray(vis)[None], rel[:, idx], NEG)


def _band_bias_sample(rel, n_cached, T):
    dist = (n_cached + np.arange(T))[:, None] - np.arange(n_cached + T)[None, :]
    idx = np.clip(dist, -(CHUNK - 1), REL_MAX) + (CHUNK - 1)
    b = rel[:, idx]
    return b[:, :, :n_cached], b[:, :, n_cached:]


def kernel(x_prompt, x_sample, cache_a_k, cache_a_v, cache_mla_ckv, cache_mla_krope, cache_sb_k, cache_sb_v,
           g_mix, w_in, g_cq, g_ckv, w_uq, w_ukv, a_rel_bias, g_out_a, g_out_mla, g_out_sb, w_out,
           g_ffn, w_up, w_down, g_final):
    bf = jnp.bfloat16
    B, S, D = x_prompt.shape
    DB, T, _ = x_sample.shape
    depth = w_in.shape[0]
    LA = cache_a_k.shape[2]
    P = cache_mla_ckv.shape[2]
    lc = min(BAND, S)
    assert T == CHUNK and P % CHUNK == 0 and LA == BAND and lc == BAND

    cache_a_k = cache_a_k.reshape(depth, DB, LA, HEADS_W)
    cache_a_v = cache_a_v.reshape(depth, DB, LA, HEADS_W)
    cache_sb_k = cache_sb_k.reshape(depth, DB, P, HEADS_W)
    cache_sb_v = cache_sb_v.reshape(depth, DB, P, HEADS_W)

    tm_s = min(TM_IN, DB * T)
    cos_p, sin_p = _rope_tables(jnp.arange(S))
    cos_s, sin_s = _rope_tables(P + jnp.arange(tm_s) % T)
    row = lambda g: g.reshape(1, -1)

    xp = x_prompt.reshape(B * S, D)
    xs = x_sample.reshape(DB * T, D)
    p_states, s_states = [], []
    for l in range(depth):
        w_in_p, w_uq_p, w_ukv_p = _prep_layer(w_in[l], w_uq[l], w_ukv[l])
        w_out_b, w_up_b, w_down_b = w_out[l].astype(bf), w_up[l].astype(bf), w_down[l].astype(bf)
        last = l == depth - 1
        lw_in = (row(g_mix[l]), w_in_p, row(g_cq[l]), row(g_ckv[l]), w_uq_p, w_ukv_p)
        lw_out = (row(g_out_a[l]), row(g_out_mla[l]), row(g_out_sb[l]), w_out_b, row(g_ffn[l]),
                  w_up_b, w_down_b, row(g_final), last)

        (qa, ka, va, kaf, vaf, ckv, kr, qc, kc, vc, kcf, vcf, qm, km, vm) = _in_proj(
            xp, *lw_in, cos_p, sin_p, S, lc)
        seq = lambda a: a.reshape(B, S, a.shape[-1])
        oa = _band_prompt(seq(qa), seq(ka), seq(va), _band_bias_prompt(a_rel_bias[l]))
        om = _mla_prompt(qm.reshape(N_HEADS, B, S, MLA_QK), km.reshape(N_HEADS, B, S, MLA_QK), seq(vm))
        osb = _sb_prompt(seq(qc), seq(kc), seq(vc))
        xp = _merge_ffn(xp, oa.reshape(B * S, -1), om.reshape(B * S, -1), osb.reshape(B * S, -1), *lw_out)
        p_states.append((kaf.reshape(B, lc, N_HEADS, HEAD_DIM), vaf.reshape(B, lc, N_HEADS, HEAD_DIM),
                         ckv.reshape(B, S, MLA_RANK), kr.reshape(B, S, MLA_ROPE),
                         kcf.reshape(B, S, N_HEADS, HEAD_DIM), vcf.reshape(B, S, N_HEADS, HEAD_DIM)))

        (qa, ka, va, kaf, vaf, ckv, kr, qc, kc, vc, kcf, vcf, qm, km, vm) = _in_proj(
            xs, *lw_in, cos_s, sin_s, T, T)
        seq = lambda a: a.reshape(DB, T, a.shape[-1])
        bias_c, bias_n = _band_bias_sample(a_rel_bias[l], LA, T)
        oa, sk, sv = _band_sample(seq(qa), seq(kaf), seq(vaf), cache_a_k, cache_a_v, l, bias_c, bias_n)
        om = _mla_sample(qm.reshape(N_HEADS, DB, T, MLA_QK), km.reshape(N_HEADS, DB, T, MLA_QK), seq(vm),
                         cache_mla_ckv, cache_mla_krope, l, w_ukv_p)
        osb = _sb_sample(seq(qc), seq(kc), seq(vc), cache_sb_k, cache_sb_v, l)
        xs = _merge_ffn(xs, oa.reshape(DB * T, -1), om.reshape(DB * T, -1), osb.reshape(DB * T, -1), *lw_out)
        s_states.append((sk.reshape(DB, LA, N_HEADS, HEAD_DIM), sv.reshape(DB, LA, N_HEADS, HEAD_DIM),
                         ckv.reshape(DB, T, MLA_RANK), kr.reshape(DB, T, MLA_ROPE),
                         kcf.reshape(DB, T, N_HEADS, HEAD_DIM), vcf.reshape(DB, T, N_HEADS, HEAD_DIM)))

    p_out = [jnp.stack(t, axis=0) for t in zip(*p_states)]
    s_out = [jnp.stack(t, axis=0) for t in zip(*s_states)]
    return (xp.reshape(B, S, D), xs.reshape(DB, T, D), *p_out, *s_out)
```

```python
import functools
import math

import numpy as np
import jax
import jax.numpy as jnp
from jax import lax
from jax.experimental import pallas as pl
from jax.experimental.pallas import tpu as pltpu

CHUNK = 64
HEAD_DIM = 64
N_HEADS = 4
BAND = 8 * CHUNK
REL_MAX = 128
MLA_NOPE = 128
MLA_ROPE = 64
MLA_V = 128
MLA_RANK = 256
ROPE_THETA = 10000.0
EPS = 1e-6
NEG = -1e30

HEADS_W = N_HEADS * HEAD_DIM
MLA_QK = 256
ROPE_PAD = 128

TQ_BAND = 256
T_MLA = 512
HP_MLA = 2
TQ_SB = 512
TK_SB = 256
LOG2E = math.log2(math.e)
TM_IN = 512
TM_FFN = 512
TF_FFN = 1024
VMEM_LIMIT = 56 * 1024 * 1024

_DN_T = (((1,), (1,)), ((), ()))


def _params(sem):
    return pltpu.CompilerParams(dimension_semantics=sem, vmem_limit_bytes=VMEM_LIMIT)


def _rms(x, g):
    return x * lax.rsqrt(jnp.mean(x * x, axis=-1, keepdims=True) + EPS) * g


def _head_mask(width, h):
    lane = lax.broadcasted_iota(jnp.int32, (1, width), 1)
    return (lane // HEAD_DIM) == h


def _in_proj_kernel(x_ref, g_ref, w_ref, gcq_ref, gckv_ref, wuq_ref, wukv_ref, cos_ref, sin_ref,
                    qa_ref, ka_ref, va_ref, kaf_ref, vaf_ref, ckv_ref, kr_ref,
                    qc_ref, kc_ref, vc_ref, kcf_ref, vcf_ref, qm_ref, km_ref, vm_ref, vmt_ref):
    bf = jnp.bfloat16
    h = _rms(x_ref[...], g_ref[...]).astype(bf)
    p = jnp.dot(h, w_ref[...], preferred_element_type=jnp.float32)
    W = HEADS_W
    qa_ref[...] = (p[:, 0:W] * HEAD_DIM ** -0.5).astype(bf)
    ka = p[:, W:2 * W]
    va = p[:, 2 * W:3 * W]
    ka_ref[...] = ka.astype(bf)
    va_ref[...] = va.astype(bf)
    kaf_ref[...] = ka
    vaf_ref[...] = va
    cq = _rms(p[:, 3 * W:4 * W], gcq_ref[...])
    ckv = _rms(p[:, 4 * W:5 * W], gckv_ref[...])
    ckv_ref[...] = ckv
    qc_ref[...] = (p[:, 5 * W:6 * W] * (HEAD_DIM ** -0.5 * LOG2E)).astype(bf)
    kc = p[:, 6 * W:7 * W]
    vc = p[:, 7 * W:8 * W]
    kc_ref[...] = kc.astype(bf)
    vc_ref[...] = vc.astype(bf)
    kcf_ref[...] = kc
    vcf_ref[...] = vc
    cos = cos_ref[...]
    sin = sin_ref[...]
    kr = p[:, 8 * W:8 * W + ROPE_PAD] * cos + p[:, 8 * W + ROPE_PAD:8 * W + 2 * ROPE_PAD] * sin
    kr_ref[...] = kr[:, :MLA_ROPE]
    krb = kr.astype(bf)
    q = jnp.dot(cq.astype(bf), wuq_ref[...], preferred_element_type=jnp.float32)
    kv = jnp.dot(ckv.astype(bf), wukv_ref[...], preferred_element_type=jnp.float32)
    scale = (MLA_NOPE + MLA_ROPE) ** -0.5 * LOG2E
    nq = N_HEADS * MLA_NOPE
    for hh in range(N_HEADS):
        qn = q[:, hh * MLA_NOPE:(hh + 1) * MLA_NOPE]
        qp = q[:, nq + hh * ROPE_PAD:nq + (hh + 1) * ROPE_PAD]
        qs = q[:, nq + (N_HEADS + hh) * ROPE_PAD:nq + (N_HEADS + hh + 1) * ROPE_PAD]
        qm_ref[hh, :, 0:MLA_NOPE] = (qn * scale).astype(bf)
        qm_ref[hh, :, MLA_NOPE:MLA_QK] = ((qp * cos + qs * sin) * scale).astype(bf)
        km_ref[hh, :, 0:MLA_NOPE] = kv[:, hh * MLA_NOPE:(hh + 1) * MLA_NOPE].astype(bf)
        km_ref[hh, :, MLA_NOPE:MLA_QK] = krb
    vm_ref[...] = kv[:, nq:].astype(bf)
    vmt_ref[...] = kv[:, nq:].T.astype(bf)


def _in_proj(x, g, w_in, g_cq, g_ckv, w_uq, w_ukv, cos, sin, rows_per_seq, tail_rows):
    N, D = x.shape
    tm = min(TM_IN, N)
    assert N % tm == 0 and cos.shape[0] % tm == 0
    n_tab = cos.shape[0] // tm
    n_seq = N // rows_per_seq
    if tail_rows == rows_per_seq:
        tail_map = lambda i: (i, 0)
    else:
        assert tail_rows % tm == 0 and rows_per_seq % tm == 0
        tiles_per_seq = rows_per_seq // tm
        tail_tiles = tail_rows // tm

        def tail_map(i):
            b = i // tiles_per_seq
            t = i % tiles_per_seq
            return (b * tail_tiles + jnp.maximum(t - (tiles_per_seq - tail_tiles), 0), 0)

    row = lambda w: pl.BlockSpec((tm, w), lambda i: (i, 0))
    full = lambda a: pl.BlockSpec(a.shape, lambda i: (0,) * a.ndim)
    tab = pl.BlockSpec((tm, ROPE_PAD), lambda i: (i % n_tab, 0))
    tail = pl.BlockSpec((tm, HEADS_W), tail_map)
    heads = pl.BlockSpec((N_HEADS, tm, MLA_QK), lambda i: (0, i, 0))
    bf, f32 = jnp.bfloat16, jnp.float32
    sds = jax.ShapeDtypeStruct
    out_shape = [
        sds((N, HEADS_W), bf), sds((N, HEADS_W), bf), sds((N, HEADS_W), bf),
        sds((n_seq * tail_rows, HEADS_W), f32), sds((n_seq * tail_rows, HEADS_W), f32),
        sds((N, MLA_RANK), f32), sds((N, MLA_ROPE), f32),
        sds((N, HEADS_W), bf), sds((N, HEADS_W), bf), sds((N, HEADS_W), bf),
        sds((N, HEADS_W), f32), sds((N, HEADS_W), f32),
        sds((N_HEADS, N, MLA_QK), bf), sds((N_HEADS, N, MLA_QK), bf),
        sds((N, N_HEADS * MLA_V), bf),
        sds((N // tm, N_HEADS * MLA_V, tm), bf),
    ]
    out_specs = [row(HEADS_W), row(HEADS_W), row(HEADS_W), tail, tail,
                 row(MLA_RANK), row(MLA_ROPE),
                 row(HEADS_W), row(HEADS_W), row(HEADS_W), row(HEADS_W), row(HEADS_W),
                 heads, heads, row(N_HEADS * MLA_V),
                 pl.BlockSpec((None, N_HEADS * MLA_V, tm), lambda i: (i, 0, 0))]
    return pl.pallas_call(
        _in_proj_kernel,
        grid=(N // tm,),
        in_specs=[row(D), full(g), full(w_in), full(g_cq), full(g_ckv), full(w_uq), full(w_ukv), tab, tab],
        out_specs=out_specs,
        out_shape=out_shape,
        compiler_params=_params(("arbitrary",)),
        name="in_proj",
    )(x, g, w_in, g_cq, g_ckv, w_uq, w_ukv, cos, sin)


def _band_heads(q, kparts, vparts, biases, valid):
    out = jnp.zeros((q.shape[0], HEADS_W), jnp.float32)
    for h in range(N_HEADS):
        hm = _head_mask(HEADS_W, h)
        qh = jnp.where(hm, q, jnp.zeros_like(q))
        ss = []
        for i, k in enumerate(kparts):
            s = lax.dot_general(qh, k, _DN_T, preferred_element_type=jnp.float32) + biases[i][h]
            if valid is not None and valid[i] is not None:
                s = jnp.where(valid[i], s, NEG)
            ss.append(s)
        m = functools.reduce(jnp.maximum, [jnp.max(s, axis=-1, keepdims=True) for s in ss])
        ps = [jnp.exp(s - m) for s in ss]
        den = functools.reduce(jnp.add, [jnp.sum(p, axis=-1, keepdims=True) for p in ps])
        o = functools.reduce(jnp.add, [jnp.dot(p.astype(jnp.bfloat16), v, preferred_element_type=jnp.float32)
                                       for p, v in zip(ps, vparts)])
        out = out + jnp.where(hm, o / den, 0.0)
    return out


def _band_prompt_kernel(q_ref, k_ref, v_ref, bias_ref, o_ref):
    i = pl.program_id(1)
    T = TQ_BAND
    nprev = BAND // T
    starts = [pl.multiple_of(jnp.maximum(i - (nprev - j), 0) * T, T) for j in range(nprev + 1)]
    k = jnp.concatenate([k_ref[pl.ds(s, T), :] for s in starts], axis=0)
    v = jnp.concatenate([v_ref[pl.ds(s, T), :] for s in starts], axis=0)
    col = lax.broadcasted_iota(jnp.int32, (1, BAND + T), 1)
    valid = col >= BAND - i * T
    o_ref[...] = _band_heads(q_ref[...], [k], [v], [bias_ref], [valid])


def _band_prompt(q, k, v, bias):
    B, S, W = q.shape
    T = TQ_BAND
    assert S % T == 0 and BAND % T == 0
    return pl.pallas_call(
        _band_prompt_kernel,
        grid=(B, S // T),
        in_specs=[pl.BlockSpec((None, T, W), lambda b, i: (b, i, 0)),
                  pl.BlockSpec((None, S, W), lambda b, i: (b, 0, 0)),
                  pl.BlockSpec((None, S, W), lambda b, i: (b, 0, 0)),
                  pl.BlockSpec(bias.shape, lambda b, i: (0, 0, 0))],
        out_specs=pl.BlockSpec((None, T, W), lambda b, i: (b, i, 0)),
        out_shape=jax.ShapeDtypeStruct((B, S, W), jnp.float32),
        compiler_params=_params(("arbitrary", "arbitrary")),
        name="band_prompt",
    )(q, k, v, bias)


def _band_sample_kernel(q_ref, kn_ref, vn_ref, ck_ref, cv_ref, bc_ref, bn_ref, o_ref, sk_ref, sv_ref):
    bf = jnp.bfloat16
    ck, cv, kn, vn = ck_ref[...], cv_ref[...], kn_ref[...], vn_ref[...]
    o_ref[...] = _band_heads(q_ref[...], [ck.astype(bf), kn.astype(bf)], [cv.astype(bf), vn.astype(bf)],
                             [bc_ref, bn_ref], None)
    n_keep = ck.shape[0] - kn.shape[0]
    sk_ref[0:n_keep, :] = ck[kn.shape[0]:, :]
    sk_ref[n_keep:, :] = kn
    sv_ref[0:n_keep, :] = cv[vn.shape[0]:, :]
    sv_ref[n_keep:, :] = vn


def _band_sample(q, k_new, v_new, cache_k, cache_v, layer, bias_c, bias_n):
    B, T, W = q.shape
    LA = cache_k.shape[2]
    new = pl.BlockSpec((None, T, W), lambda b: (b, 0, 0))
    cache = pl.BlockSpec((None, None, LA, W), lambda b: (layer, b, 0, 0))
    roll = pl.BlockSpec((None, LA, W), lambda b: (b, 0, 0))
    full = lambda a: pl.BlockSpec(a.shape, lambda b: (0,) * a.ndim)
    return pl.pallas_call(
        _band_sample_kernel,
        grid=(B,),
        in_specs=[new, new, new, cache, cache, full(bias_c), full(bias_n)],
        out_specs=[new, roll, roll],
        out_shape=[jax.ShapeDtypeStruct((B, T, W), jnp.float32),
                   jax.ShapeDtypeStruct((B, LA, W), jnp.float32),
                   jax.ShapeDtypeStruct((B, LA, W), jnp.float32)],
        compiler_params=_params(("arbitrary",)),
        name="band_sample",
    )(q, k_new, v_new, cache_k, cache_v, bias_c, bias_n)


def _mla_prompt_kernel(q_ref, k_ref, vt_ref, o_ref, m_ref, l_ref, acc_ref):
    i = pl.program_id(2)
    T = T_MLA
    m_ref[...] = jnp.full(m_ref.shape, NEG, jnp.float32)
    l_ref[...] = jnp.zeros(l_ref.shape, jnp.float32)
    acc_ref[...] = jnp.zeros(acc_ref.shape, jnp.float32)

    def block(j, mask):
        start = pl.multiple_of(j * T, T)
        for hh in range(HP_MLA):
            k = k_ref[hh, pl.ds(start, T), :]
            vt = vt_ref[j, hh * MLA_V:(hh + 1) * MLA_V, :]
            st = lax.dot_general(k, q_ref[hh], _DN_T, preferred_element_type=jnp.float32)
            if mask is not None:
                st = jnp.where(mask, st, NEG)
            m_old = m_ref[hh]
            m_new = jnp.maximum(m_old, jnp.max(st, axis=0, keepdims=True))
            alpha = jnp.exp2(m_old - m_new)
            pt = jnp.exp2(st - m_new)
            l_ref[hh] = alpha * l_ref[hh] + jnp.sum(pt, axis=0, keepdims=True)
            acc_ref[hh] = alpha * acc_ref[hh] + jnp.dot(vt, pt.astype(jnp.bfloat16),
                                                        preferred_element_type=jnp.float32)
            m_ref[hh] = m_new

    def body(j, c):
        block(j, None)
        return c

    lax.fori_loop(0, i, body, 0)
    r = lax.broadcasted_iota(jnp.int32, (T, T), 0) // CHUNK
    c = lax.broadcasted_iota(jnp.int32, (T, T), 1) // CHUNK
    block(i, r <= c)
    for hh in range(HP_MLA):
        o_ref[:, hh * MLA_V:(hh + 1) * MLA_V] = (acc_ref[hh] / l_ref[hh]).T


def _mla_prompt(qm, km, vmt):
    H, B, S, E = qm.shape
    T, HP = T_MLA, HP_MLA
    assert S % T == 0 and H % HP == 0 and vmt.shape == (B, S // T, H * MLA_V, T)
    return pl.pallas_call(
        _mla_prompt_kernel,
        grid=(B, H // HP, S // T),
        in_specs=[pl.BlockSpec((HP, None, T, E), lambda b, h, i: (h, b, i, 0)),
                  pl.BlockSpec((HP, None, S, E), lambda b, h, i: (h, b, 0, 0)),
                  pl.BlockSpec((None, S // T, HP * MLA_V, T), lambda b, h, i: (b, 0, h, 0))],
        out_specs=pl.BlockSpec((None, T, HP * MLA_V), lambda b, h, i: (b, i, h)),
        out_shape=jax.ShapeDtypeStruct((B, S, H * MLA_V), jnp.float32),
        scratch_shapes=[pltpu.VMEM((HP, 1, T), jnp.float32), pltpu.VMEM((HP, 1, T), jnp.float32),
                        pltpu.VMEM((HP, MLA_V, T), jnp.float32)],
        compiler_params=_params(("arbitrary", "arbitrary", "arbitrary")),
        name="mla_prompt",
    )(qm, km, vmt)


def _mla_sample_kernel(q_ref, kn_ref, vn_ref, ckv_ref, ckr_ref, wukv_ref, o_ref, kv_ref):
    bf = jnp.bfloat16
    P = ckv_ref.shape[0]
    step = min(512, P)
    for r in range(0, P, step):
        kv_ref[r:r + step, :] = jnp.dot(ckv_ref[r:r + step, :].astype(bf), wukv_ref[...],
                                        preferred_element_type=jnp.float32).astype(bf)
    ckr = ckr_ref[...].astype(bf)
    nk = N_HEADS * MLA_NOPE
    for h in range(N_HEADS):
        q = q_ref[h]
        s_c = (lax.dot_general(q[:, :MLA_NOPE], kv_ref[:, h * MLA_NOPE:(h + 1) * MLA_NOPE], _DN_T,
                               preferred_element_type=jnp.float32)
               + lax.dot_general(q[:, MLA_NOPE:MLA_NOPE + MLA_ROPE], ckr, _DN_T,
                                 preferred_element_type=jnp.float32))
        s_n = lax.dot_general(q, kn_ref[h], _DN_T, preferred_element_type=jnp.float32)
        m = jnp.maximum(jnp.max(s_c, axis=-1, keepdims=True), jnp.max(s_n, axis=-1, keepdims=True))
        p_c = jnp.exp2(s_c - m)
        p_n = jnp.exp2(s_n - m)
        den = jnp.sum(p_c, axis=-1, keepdims=True) + jnp.sum(p_n, axis=-1, keepdims=True)
        o = (jnp.dot(p_c.astype(bf), kv_ref[:, nk + h * MLA_V:nk + (h + 1) * MLA_V],
                     preferred_element_type=jnp.float32)
             + jnp.dot(p_n.astype(bf), vn_ref[:, h * MLA_V:(h + 1) * MLA_V],
                       preferred_element_type=jnp.float32))
        o_ref[:, h * MLA_V:(h + 1) * MLA_V] = o / den


def _mla_sample(qm, km, vm, cache_ckv, cache_kr, layer, w_ukv):
    H, B, T, E = qm.shape
    P = cache_ckv.shape[2]
    heads = pl.BlockSpec((H, None, T, E), lambda b: (0, b, 0, 0))
    return pl.pallas_call(
        _mla_sample_kernel,
        grid=(B,),
        in_specs=[heads, heads,
                  pl.BlockSpec((None, T, H * MLA_V), lambda b: (b, 0, 0)),
                  pl.BlockSpec((None, None, P, MLA_RANK), lambda b: (layer, b, 0, 0)),
                  pl.BlockSpec((None, None, P, MLA_ROPE), lambda b: (layer, b, 0, 0)),
                  pl.BlockSpec(w_ukv.shape, lambda b: (0, 0))],
        out_specs=pl.BlockSpec((None, T, H * MLA_V), lambda b: (b, 0, 0)),
        out_shape=jax.ShapeDtypeStruct((B, T, H * MLA_V), jnp.float32),
        scratch_shapes=[pltpu.VMEM((P, w_ukv.shape[1]), jnp.bfloat16)],
        compiler_params=_params(("arbitrary",)),
        name="mla_sample",
    )(qm, km, vm, cache_ckv, cache_kr, w_ukv)


def _neg_tri(n):
    r = lax.broadcasted_iota(jnp.int32, (n, n), 0)
    c = lax.broadcasted_iota(jnp.int32, (n, n), 1)
    return jnp.where(r >= c, -1.0, 0.0).astype(jnp.bfloat16)


def _sb_init(q, qh_ref, acc_ref, car_ref):
    for h in range(N_HEADS):
        qh_ref[h] = jnp.where(_head_mask(HEADS_W, h), q, jnp.zeros_like(q))
    acc_ref[...] = jnp.zeros(acc_ref.shape, jnp.float32)
    car_ref[...] = jnp.zeros(car_ref.shape, jnp.float32)


def _sb_tile(qh_ref, k, v, acc_ref, car_ref, ntri, mask):
    for h in range(N_HEADS):
        z = lax.dot_general(qh_ref[h], k, _DN_T, preferred_element_type=jnp.float32)
        neg_abs = lax.bitcast_convert_type(lax.bitcast_convert_type(z, jnp.uint32) | jnp.uint32(0x80000000),
                                           jnp.float32)
        sp = jnp.maximum(z, 0.0) + jnp.log2(1.0 + jnp.exp2(neg_abs))
        if mask is not None:
            sp = jnp.where(mask, sp, 0.0)
        inner = jnp.dot(sp.astype(jnp.bfloat16), ntri, preferred_element_type=jnp.float32)
        a = jnp.minimum(jnp.exp2(z + inner + car_ref[h]), 1.0)
        if mask is not None:
            a = jnp.where(mask, a, 0.0)
        acc_ref[h] += jnp.dot(a.astype(jnp.bfloat16), v, preferred_element_type=jnp.float32)
        car_ref[h] += inner[:, 0:1]


def _sb_finish(acc_ref):
    return functools.reduce(jnp.add, [jnp.where(_head_mask(HEADS_W, h), acc_ref[h], 0.0)
                                      for h in range(N_HEADS)])


def _sb_prompt_kernel(q_ref, k_ref, v_ref, o_ref, qh_ref, acc_ref, car_ref):
    i = pl.program_id(1)
    TQ, TK = TQ_SB, TK_SB
    nd = TQ // TK
    _sb_init(q_ref[...], qh_ref, acc_ref, car_ref)
    ntri = _neg_tri(TK)
    row = i * TQ + lax.broadcasted_iota(jnp.int32, (TQ, TK), 0)
    col = lax.broadcasted_iota(jnp.int32, (TQ, TK), 1)
    for d in range(nd - 1, -1, -1):
        start = pl.multiple_of((i * nd + d) * TK, TK)
        _sb_tile(qh_ref, k_ref[pl.ds(start, TK), :], v_ref[pl.ds(start, TK), :], acc_ref, car_ref, ntri,
                 start + col < row)

    def body(n, c):
        start = pl.multiple_of((i * nd - 1 - n) * TK, TK)
        _sb_tile(qh_ref, k_ref[pl.ds(start, TK), :], v_ref[pl.ds(start, TK), :], acc_ref, car_ref, ntri, None)
        return c

    lax.fori_loop(0, i * nd, body, 0)
    o_ref[...] = _sb_finish(acc_ref)


def _sb_prompt(q, k, v):
    B, S, W = q.shape
    TQ = TQ_SB
    assert S % TQ == 0 and TQ % TK_SB == 0
    return pl.pallas_call(
        _sb_prompt_kernel,
        grid=(B, S // TQ),
        in_specs=[pl.BlockSpec((None, TQ, W), lambda b, i: (b, i, 0)),
                  pl.BlockSpec((None, S, W), lambda b, i: (b, 0, 0)),
                  pl.BlockSpec((None, S, W), lambda b, i: (b, 0, 0))],
        out_specs=pl.BlockSpec((None, TQ, W), lambda b, i: (b, i, 0)),
        out_shape=jax.ShapeDtypeStruct((B, S, W), jnp.float32),
        scratch_shapes=[pltpu.VMEM((N_HEADS, TQ, W), jnp.bfloat16), pltpu.VMEM((N_HEADS, TQ, W), jnp.float32),
                        pltpu.VMEM((N_HEADS, TQ, 1), jnp.float32)],
        compiler_params=_params(("arbitrary", "arbitrary")),
        name="sb_prompt",
    )(q, k, v)


def _sb_sample_kernel(q_ref, kn_ref, vn_ref, ck_ref, cv_ref, o_ref, kb_ref, vb_ref, qh_ref, acc_ref, car_ref):
    bf = jnp.bfloat16
    T = q_ref.shape[0]
    P = ck_ref.shape[0]
    TK = min(TK_SB, P)
    kb_ref[...] = ck_ref[...].astype(bf)
    vb_ref[...] = cv_ref[...].astype(bf)
    _sb_init(q_ref[...], qh_ref, acc_ref, car_ref)
    r = lax.broadcasted_iota(jnp.int32, (T, T), 0)
    c = lax.broadcasted_iota(jnp.int32, (T, T), 1)
    _sb_tile(qh_ref, kn_ref[...], vn_ref[...], acc_ref, car_ref, _neg_tri(T), c < r)
    ntri = _neg_tri(TK)

    def body(n, carry):
        start = pl.multiple_of(P - (n + 1) * TK, TK)
        _sb_tile(qh_ref, kb_ref[pl.ds(start, TK), :], vb_ref[pl.ds(start, TK), :], acc_ref, car_ref, ntri, None)
        return carry

    lax.fori_loop(0, P // TK, body, 0)
    o_ref[...] = _sb_finish(acc_ref)


def _sb_sample(q, k_new, v_new, cache_k, cache_v, layer):
    B, T, W = q.shape
    P = cache_k.shape[2]
    assert P % min(TK_SB, P) == 0
    new = pl.BlockSpec((None, T, W), lambda b: (b, 0, 0))
    cache = pl.BlockSpec((None, None, P, W), lambda b: (layer, b, 0, 0))
    return pl.pallas_call(
        _sb_sample_kernel,
        grid=(B,),
        in_specs=[new, new, new, cache, cache],
        out_specs=new,
        out_shape=jax.ShapeDtypeStruct((B, T, W), jnp.float32),
        scratch_shapes=[pltpu.VMEM((P, W), jnp.bfloat16), pltpu.VMEM((P, W), jnp.bfloat16),
                        pltpu.VMEM((N_HEADS, T, W), jnp.bfloat16), pltpu.VMEM((N_HEADS, T, W), jnp.float32),
                        pltpu.VMEM((N_HEADS, T, 1), jnp.float32)],
        compiler_params=_params(("arbitrary",)),
        name="sb_sample",
    )(q, k_new, v_new, cache_k, cache_v)


def _merge_ffn_kernel(final, x_ref, oa_ref, om_ref, os_ref, goa_ref, gom_ref, gos_ref, wout_ref,
                      gffn_ref, wup_ref, wdown_ref, gfin_ref, y_ref, x1_ref, h_ref, acc_ref):
    bf = jnp.bfloat16
    j = pl.program_id(1)

    @pl.when(j == 0)
    def _():
        cat = jnp.concatenate([_rms(oa_ref[...], goa_ref[...]).astype(bf),
                               _rms(om_ref[...], gom_ref[...]).astype(bf),
                               _rms(os_ref[...], gos_ref[...]).astype(bf)], axis=-1)
        x1 = x_ref[...] + jnp.dot(cat, wout_ref[...], preferred_element_type=jnp.float32)
        x1_ref[...] = x1
        h_ref[...] = _rms(x1, gffn_ref[...]).astype(bf)
        acc_ref[...] = jnp.zeros(acc_ref.shape, jnp.float32)

    u = jnp.maximum(jnp.dot(h_ref[...], wup_ref[...], preferred_element_type=jnp.float32), 0.0)
    acc_ref[...] += jnp.dot((u * u).astype(bf), wdown_ref[...], preferred_element_type=jnp.float32)

    @pl.when(j == pl.num_programs(1) - 1)
    def _():
        y = x1_ref[...] + acc_ref[...]
        y_ref[...] = _rms(y, gfin_ref[...]) if final else y


def _merge_ffn(x, oa, om, osb, g_oa, g_om, g_os, w_out, g_ffn, w_up, w_down, g_final, final):
    N, D = x.shape
    F = w_up.shape[1]
    tm = min(TM_FFN, N)
    tf = min(TF_FFN, F)
    assert N % tm == 0 and F % tf == 0
    row = lambda w: pl.BlockSpec((tm, w), lambda i, j: (i, 0))
    full = lambda a: pl.BlockSpec(a.shape, lambda i, j: (0,) * a.ndim)
    return pl.pallas_call(
        functools.partial(_merge_ffn_kernel, final),
        grid=(N // tm, F // tf),
        in_specs=[row(D), row(oa.shape[1]), row(om.shape[1]), row(osb.shape[1]),
                  full(g_oa), full(g_om), full(g_os), full(w_out), full(g_ffn),
                  pl.BlockSpec((D, tf), lambda i, j: (0, j)),
                  pl.BlockSpec((tf, D), lambda i, j: (j, 0)),
                  full(g_final)],
        out_specs=row(D),
        out_shape=jax.ShapeDtypeStruct((N, D), jnp.float32),
        scratch_shapes=[pltpu.VMEM((tm, D), jnp.float32), pltpu.VMEM((tm, D), jnp.bfloat16),
                        pltpu.VMEM((tm, D), jnp.float32)],
        compiler_params=_params(("arbitrary", "arbitrary")),
        name="merge_ffn",
    )(x, oa, om, osb, g_oa, g_om, g_os, w_out, g_ffn, w_up, w_down, g_final)


def _swap_halves(w):
    half = w.shape[-1] // 2
    return jnp.concatenate([w[..., half:], w[..., :half]], axis=-1)


def _pad_cols(w, width):
    return jnp.pad(w, [(0, 0)] * (w.ndim - 1) + [(0, width - w.shape[-1])])


def _prep_layer(w_in, w_uq, w_ukv):
    bf = jnp.bfloat16
    W = HEADS_W
    a_end = 3 * W
    cq_end = a_end + MLA_RANK
    ckv_end = cq_end + MLA_RANK
    kr_end = ckv_end + MLA_ROPE
    w_kr = w_in[:, ckv_end:kr_end]
    w_in_p = jnp.concatenate([w_in[:, :ckv_end], w_in[:, kr_end:],
                              _pad_cols(w_kr, ROPE_PAD), _pad_cols(_swap_halves(w_kr), ROPE_PAD)], axis=1)
    qn = w_uq[:, :, :MLA_NOPE].reshape(MLA_RANK, N_HEADS * MLA_NOPE)
    qp = w_uq[:, :, MLA_NOPE:]
    w_uq_p = jnp.concatenate([qn, _pad_cols(qp, ROPE_PAD).reshape(MLA_RANK, -1),
                              _pad_cols(_swap_halves(qp), ROPE_PAD).reshape(MLA_RANK, -1)], axis=1)
    w_ukv_p = jnp.concatenate([w_ukv[:, :, :MLA_NOPE].reshape(MLA_RANK, -1),
                               w_ukv[:, :, MLA_NOPE:].reshape(MLA_RANK, -1)], axis=1)
    return w_in_p.astype(bf), w_uq_p.astype(bf), w_ukv_p.astype(bf)


def _rope_tables(pos):
    half = MLA_ROPE // 2
    inv = ROPE_THETA ** (-jnp.arange(half, dtype=jnp.float32) / half)
    ang = pos.astype(jnp.float32)[:, None] * inv[None, :]
    cos, sin = jnp.cos(ang), jnp.sin(ang)
    return (_pad_cols(jnp.concatenate([cos, cos], axis=-1), ROPE_PAD),
            _pad_cols(jnp.concatenate([-sin, sin], axis=-1), ROPE_PAD))


def _band_bias_chunk(rel):
    R = rel.shape[-1]
    assert R == CHUNK + REL_MAX
    K = BAND + CHUNK
    L = K + CHUNK - 1
    lead = rel.shape[:-1]
    g = jnp.concatenate([rel, jnp.broadcast_to(rel[..., R - 1:], lead + (L - R,))], axis=-1)
    g = _pad_cols(jnp.flip(g, axis=-1), L + 1)
    m = jnp.tile(g, (1,) * len(lead) + (CHUNK,))[..., :CHUNK * L].reshape(lead + (CHUNK, L))
    return m[..., CHUNK - 1:CHUNK - 1 + K]


def _band_bias_prompt(chunk_bias):
    n = TQ_BAND // CHUNK
    lead = [(0, 0)] * (chunk_bias.ndim - 1)
    rows = [jnp.pad(chunk_bias, lead + [(CHUNK * c, CHUNK * (n - 1 - c))], constant_values=NEG) for c in range(n)]
    return jnp.concatenate(rows, axis=-2)


def kernel(x_prompt, x_sample, cache_a_k, cache_a_v, cache_mla_ckv, cache_mla_krope, cache_sb_k, cache_sb_v,
           g_mix, w_in, g_cq, g_ckv, w_uq, w_ukv, a_rel_bias, g_out_a, g_out_mla, g_out_sb, w_out,
           g_ffn, w_up, w_down, g_final):
    bf = jnp.bfloat16
    B, S, D = x_prompt.shape
    DB, T, _ = x_sample.shape
    depth = w_in.shape[0]
    LA = cache_a_k.shape[2]
    P = cache_mla_ckv.shape[2]
    lc = min(BAND, S)
    assert T == CHUNK and P % CHUNK == 0 and LA == BAND and lc == BAND

    cache_a_k = cache_a_k.reshape(depth, DB, LA, HEADS_W)
    cache_a_v = cache_a_v.reshape(depth, DB, LA, HEADS_W)
    cache_sb_k = cache_sb_k.reshape(depth, DB, P, HEADS_W)
    cache_sb_v = cache_sb_v.reshape(depth, DB, P, HEADS_W)

    tm_s = min(TM_IN, DB * T)
    cos_p, sin_p = _rope_tables(jnp.arange(S))
    cos_s, sin_s = _rope_tables(P + jnp.arange(tm_s) % T)
    row = lambda g: g.reshape(1, -1)
    chunk_bias = _band_bias_chunk(a_rel_bias)
    bias_p = _band_bias_prompt(chunk_bias)

    xp = x_prompt.reshape(B * S, D)
    xs = x_sample.reshape(DB * T, D)
    p_states, s_states = [], []
    for l in range(depth):
        w_in_p, w_uq_p, w_ukv_p = _prep_layer(w_in[l], w_uq[l], w_ukv[l])
        w_out_b, w_up_b, w_down_b = w_out[l].astype(bf), w_up[l].astype(bf), w_down[l].astype(bf)
        last = l == depth - 1
        lw_in = (row(g_mix[l]), w_in_p, row(g_cq[l]), row(g_ckv[l]), w_uq_p, w_ukv_p)
        lw_out = (row(g_out_a[l]), row(g_out_mla[l]), row(g_out_sb[l]), w_out_b, row(g_ffn[l]),
                  w_up_b, w_down_b, row(g_final), last)

        (qa, ka, va, kaf, vaf, ckv, kr, qc, kc, vc, kcf, vcf, qm, km, _, vmt) = _in_proj(
            xp, *lw_in, cos_p, sin_p, S, lc)
        seq = lambda a: a.reshape(B, S, a.shape[-1])
        oa = _band_prompt(seq(qa), seq(ka), seq(va), bias_p[l])
        om = _mla_prompt(qm.reshape(N_HEADS, B, S, MLA_QK), km.reshape(N_HEADS, B, S, MLA_QK),
                         vmt.reshape(B, S // T_MLA, N_HEADS * MLA_V, T_MLA))
        osb = _sb_prompt(seq(qc), seq(kc), seq(vc))
        xp = _merge_ffn(xp, oa.reshape(B * S, -1), om.reshape(B * S, -1), osb.reshape(B * S, -1), *lw_out)
        p_states.append((kaf.reshape(B, lc, N_HEADS, HEAD_DIM), vaf.reshape(B, lc, N_HEADS, HEAD_DIM),
                         ckv.reshape(B, S, MLA_RANK), kr.reshape(B, S, MLA_ROPE),
                         kcf.reshape(B, S, N_HEADS, HEAD_DIM), vcf.reshape(B, S, N_HEADS, HEAD_DIM)))

        (qa, ka, va, kaf, vaf, ckv, kr, qc, kc, vc, kcf, vcf, qm, km, vm, _) = _in_proj(
            xs, *lw_in, cos_s, sin_s, T, T)
        seq = lambda a: a.reshape(DB, T, a.shape[-1])
        bias_c, bias_n = chunk_bias[l, :, :, :LA], chunk_bias[l, :, :, LA:]
        oa, sk, sv = _band_sample(seq(qa), seq(kaf), seq(vaf), cache_a_k, cache_a_v, l, bias_c, bias_n)
        om = _mla_sample(qm.reshape(N_HEADS, DB, T, MLA_QK), km.reshape(N_HEADS, DB, T, MLA_QK), seq(vm),
                         cache_mla_ckv, cache_mla_krope, l, w_ukv_p)
        osb = _sb_sample(seq(qc), seq(kc), seq(vc), cache_sb_k, cache_sb_v, l)
        xs = _merge_ffn(xs, oa.reshape(DB * T, -1), om.reshape(DB * T, -1), osb.reshape(DB * T, -1), *lw_out)
        s_states.append((sk.reshape(DB, LA, N_HEADS, HEAD_DIM), sv.reshape(DB, LA, N_HEADS, HEAD_DIM),
                         ckv.reshape(DB, T, MLA_RANK), kr.reshape(DB, T, MLA_ROPE),
                         kcf.reshape(DB, T, N_HEADS, HEAD_DIM), vcf.reshape(DB, T, N_HEADS, HEAD_DIM)))

    p_out = [jnp.stack(t, axis=0) for t in zip(*p_states)]
    s_out = [jnp.stack(t, axis=0) for t in zip(*s_states)]
    return (xp.reshape(B, S, D), xs.reshape(DB, T, D), *p_out, *s_out)
```

```python
import functools
import math

import numpy as np
import jax
import jax.numpy as jnp
from jax import lax
from jax.experimental import pallas as pl
from jax.experimental.pallas import tpu as pltpu

CHUNK = 64
HEAD_DIM = 64
N_HEADS = 4
BAND = 8 * CHUNK
REL_MAX = 128
MLA_NOPE = 128
MLA_ROPE = 64
MLA_V = 128
MLA_RANK = 256
ROPE_THETA = 10000.0
EPS = 1e-6
NEG = -1e30

HEADS_W = N_HEADS * HEAD_DIM
MLA_QK = 256
ROPE_PAD = 128

TQ_BAND = 256
TM_IN = 512
T_MLA = TM_IN
HP_MLA = 2
TQ_SB = 512
TK_SB = 256
LOG2E = math.log2(math.e)
TM_FFN = 512
TF_FFN = 1024
VMEM_LIMIT = 56 * 1024 * 1024

_DN_T = (((1,), (1,)), ((), ()))


def _params(sem, flags=None):
    return pltpu.CompilerParams(dimension_semantics=sem, vmem_limit_bytes=VMEM_LIMIT, flags=flags)


def _rms(x, g):
    return x * lax.rsqrt(jnp.mean(x * x, axis=-1, keepdims=True) + EPS) * g


def _head_mask(width, h):
    lane = lax.broadcasted_iota(jnp.int32, (1, width), 1)
    return (lane // HEAD_DIM) == h


def _in_proj_kernel(x_ref, g_ref, w_ref, gcq_ref, gckv_ref, wuq_ref, wukv_ref, cos_ref, sin_ref,
                    qa_ref, ka_ref, va_ref, kaf_ref, vaf_ref, ckv_ref, kr_ref,
                    qc_ref, kc_ref, vc_ref, kcf_ref, vcf_ref, qm_ref, km_ref, vm_ref, vmt_ref):
    bf = jnp.bfloat16
    h = _rms(x_ref[...], g_ref[...]).astype(bf)
    p = jnp.dot(h, w_ref[...], preferred_element_type=jnp.float32)
    W = HEADS_W
    qa_ref[...] = (p[:, 0:W] * HEAD_DIM ** -0.5).astype(bf)
    ka = p[:, W:2 * W]
    va = p[:, 2 * W:3 * W]
    ka_ref[...] = ka.astype(bf)
    va_ref[...] = va.astype(bf)
    kaf_ref[...] = ka
    vaf_ref[...] = va
    cq = _rms(p[:, 3 * W:4 * W], gcq_ref[...])
    ckv = _rms(p[:, 4 * W:5 * W], gckv_ref[...])
    ckv_ref[...] = ckv
    qc_ref[...] = (p[:, 5 * W:6 * W] * (HEAD_DIM ** -0.5 * LOG2E)).astype(bf)
    kc = p[:, 6 * W:7 * W]
    vc = p[:, 7 * W:8 * W]
    kc_ref[...] = kc.astype(bf)
    vc_ref[...] = vc.astype(bf)
    kcf_ref[...] = kc
    vcf_ref[...] = vc
    cos = cos_ref[...]
    sin = sin_ref[...]
    kr = p[:, 8 * W:8 * W + ROPE_PAD] * cos + p[:, 8 * W + ROPE_PAD:8 * W + 2 * ROPE_PAD] * sin
    kr_ref[...] = kr[:, :MLA_ROPE]
    krb = kr.astype(bf)
    q = jnp.dot(cq.astype(bf), wuq_ref[...], preferred_element_type=jnp.float32)
    kv = jnp.dot(ckv.astype(bf), wukv_ref[...], preferred_element_type=jnp.float32)
    scale = (MLA_NOPE + MLA_ROPE) ** -0.5 * LOG2E
    nq = N_HEADS * MLA_NOPE
    for hh in range(N_HEADS):
        qn = q[:, hh * MLA_NOPE:(hh + 1) * MLA_NOPE]
        qp = q[:, nq + hh * ROPE_PAD:nq + (hh + 1) * ROPE_PAD]
        qs = q[:, nq + (N_HEADS + hh) * ROPE_PAD:nq + (N_HEADS + hh + 1) * ROPE_PAD]
        qm_ref[hh, :, 0:MLA_NOPE] = (qn * scale).astype(bf)
        qm_ref[hh, :, MLA_NOPE:MLA_QK] = ((qp * cos + qs * sin) * scale).astype(bf)
        km_ref[hh, :, 0:MLA_NOPE] = kv[:, hh * MLA_NOPE:(hh + 1) * MLA_NOPE].astype(bf)
        km_ref[hh, :, MLA_NOPE:MLA_QK] = krb
    vm_ref[...] = kv[:, nq:].astype(bf)
    vmt_ref[...] = kv[:, nq:].T.astype(bf)


def _in_proj(x, g, w_in, g_cq, g_ckv, w_uq, w_ukv, cos, sin, rows_per_seq, tail_rows):
    N, D = x.shape
    tm = min(TM_IN, N)
    assert N % tm == 0 and cos.shape[0] % tm == 0
    n_tab = cos.shape[0] // tm
    n_seq = N // rows_per_seq
    if tail_rows == rows_per_seq:
        tail_map = lambda i: (i, 0)
    else:
        assert tail_rows % tm == 0 and rows_per_seq % tm == 0
        tiles_per_seq = rows_per_seq // tm
        tail_tiles = tail_rows // tm

        def tail_map(i):
            b = i // tiles_per_seq
            t = i % tiles_per_seq
            return (b * tail_tiles + jnp.maximum(t - (tiles_per_seq - tail_tiles), 0), 0)

    row = lambda w: pl.BlockSpec((tm, w), lambda i: (i, 0))
    full = lambda a: pl.BlockSpec(a.shape, lambda i: (0,) * a.ndim)
    tab = pl.BlockSpec((tm, ROPE_PAD), lambda i: (i % n_tab, 0))
    tail = pl.BlockSpec((tm, HEADS_W), tail_map)
    heads = pl.BlockSpec((N_HEADS, tm, MLA_QK), lambda i: (0, i, 0))
    bf, f32 = jnp.bfloat16, jnp.float32
    sds = jax.ShapeDtypeStruct
    out_shape = [
        sds((N, HEADS_W), bf), sds((N, HEADS_W), bf), sds((N, HEADS_W), bf),
        sds((n_seq * tail_rows, HEADS_W), f32), sds((n_seq * tail_rows, HEADS_W), f32),
        sds((N, MLA_RANK), f32), sds((N, MLA_ROPE), f32),
        sds((N, HEADS_W), bf), sds((N, HEADS_W), bf), sds((N, HEADS_W), bf),
        sds((N, HEADS_W), f32), sds((N, HEADS_W), f32),
        sds((N_HEADS, N, MLA_QK), bf), sds((N_HEADS, N, MLA_QK), bf),
        sds((N, N_HEADS * MLA_V), bf),
        sds((N // tm, N_HEADS * MLA_V, tm), bf),
    ]
    out_specs = [row(HEADS_W), row(HEADS_W), row(HEADS_W), tail, tail,
                 row(MLA_RANK), row(MLA_ROPE),
                 row(HEADS_W), row(HEADS_W), row(HEADS_W), row(HEADS_W), row(HEADS_W),
                 heads, heads, row(N_HEADS * MLA_V),
                 pl.BlockSpec((None, N_HEADS * MLA_V, tm), lambda i: (i, 0, 0))]
    return pl.pallas_call(
        _in_proj_kernel,
        grid=(N // tm,),
        in_specs=[row(D), full(g), full(w_in), full(g_cq), full(g_ckv), full(w_uq), full(w_ukv), tab, tab],
        out_specs=out_specs,
        out_shape=out_shape,
        compiler_params=_params(("arbitrary",)),
        name="in_proj",
    )(x, g, w_in, g_cq, g_ckv, w_uq, w_ukv, cos, sin)


def _band_heads(q, kparts, vparts, biases, valid):
    out = jnp.zeros((q.shape[0], HEADS_W), jnp.float32)
    hms = [_head_mask(HEADS_W, h) for h in range(N_HEADS)]
    raw = [[lax.dot_general(jnp.where(hm, q, jnp.zeros_like(q)), k, _DN_T, preferred_element_type=jnp.float32)
            for k in kparts] for hm in hms]
    for h in range(N_HEADS):
        hm = hms[h]
        ss = []
        for i in range(len(kparts)):
            s = raw[h][i] + biases[i][h]
            if valid is not None and valid[i] is not None:
                s = jnp.where(valid[i], s, NEG)
            ss.append(s)
        m = functools.reduce(jnp.maximum, [jnp.max(s, axis=-1, keepdims=True) for s in ss])
        ps = [jnp.exp(s - m) for s in ss]
        den = functools.reduce(jnp.add, [jnp.sum(p, axis=-1, keepdims=True) for p in ps])
        o = functools.reduce(jnp.add, [jnp.dot(p.astype(jnp.bfloat16), v, preferred_element_type=jnp.float32)
                                       for p, v in zip(ps, vparts)])
        out = out + jnp.where(hm, o / den, 0.0)
    return out


def _band_prompt_kernel(q_ref, k_ref, v_ref, bias_ref, o_ref):
    i = pl.program_id(1)
    T = TQ_BAND
    nprev = BAND // T
    starts = [pl.multiple_of(jnp.maximum(i - (nprev - j), 0) * T, T) for j in range(nprev + 1)]
    k = jnp.concatenate([k_ref[pl.ds(s, T), :] for s in starts], axis=0)
    v = jnp.concatenate([v_ref[pl.ds(s, T), :] for s in starts], axis=0)
    col = lax.broadcasted_iota(jnp.int32, (1, BAND + T), 1)
    valid = col >= BAND - i * T
    o_ref[...] = _band_heads(q_ref[...], [k], [v], [bias_ref], [valid])


def _band_prompt(q, k, v, bias):
    B, S, W = q.shape
    T = TQ_BAND
    assert S % T == 0 and BAND % T == 0
    return pl.pallas_call(
        _band_prompt_kernel,
        grid=(B, S // T),
        in_specs=[pl.BlockSpec((None, T, W), lambda b, i: (b, i, 0)),
                  pl.BlockSpec((None, S, W), lambda b, i: (b, 0, 0)),
                  pl.BlockSpec((None, S, W), lambda b, i: (b, 0, 0)),
                  pl.BlockSpec(bias.shape, lambda b, i: (0, 0, 0))],
        out_specs=pl.BlockSpec((None, T, W), lambda b, i: (b, i, 0)),
        out_shape=jax.ShapeDtypeStruct((B, S, W), jnp.float32),
        compiler_params=_params(("arbitrary", "arbitrary")),
        name="band_prompt",
    )(q, k, v, bias)


def _band_sample_kernel(q_ref, kn_ref, vn_ref, ck_ref, cv_ref, bc_ref, bn_ref, o_ref, sk_ref, sv_ref):
    bf = jnp.bfloat16
    ck, cv, kn, vn = ck_ref[...], cv_ref[...], kn_ref[...], vn_ref[...]
    o_ref[...] = _band_heads(q_ref[...], [ck.astype(bf), kn.astype(bf)], [cv.astype(bf), vn.astype(bf)],
                             [bc_ref, bn_ref], None)
    n_keep = ck.shape[0] - kn.shape[0]
    sk_ref[0:n_keep, :] = ck[kn.shape[0]:, :]
    sk_ref[n_keep:, :] = kn
    sv_ref[0:n_keep, :] = cv[vn.shape[0]:, :]
    sv_ref[n_keep:, :] = vn


def _band_sample(q, k_new, v_new, cache_k, cache_v, layer, bias_c, bias_n):
    B, T, W = q.shape
    LA = cache_k.shape[2]
    new = pl.BlockSpec((None, T, W), lambda b: (b, 0, 0))
    cache = pl.BlockSpec((None, None, LA, W), lambda b: (layer, b, 0, 0))
    roll = pl.BlockSpec((None, LA, W), lambda b: (b, 0, 0))
    full = lambda a: pl.BlockSpec(a.shape, lambda b: (0,) * a.ndim)
    return pl.pallas_call(
        _band_sample_kernel,
        grid=(B,),
        in_specs=[new, new, new, cache, cache, full(bias_c), full(bias_n)],
        out_specs=[new, roll, roll],
        out_shape=[jax.ShapeDtypeStruct((B, T, W), jnp.float32),
                   jax.ShapeDtypeStruct((B, LA, W), jnp.float32),
                   jax.ShapeDtypeStruct((B, LA, W), jnp.float32)],
        compiler_params=_params(("arbitrary",)),
        name="band_sample",
    )(q, k_new, v_new, cache_k, cache_v, bias_c, bias_n)


def _mla_prompt_kernel(q_ref, k_ref, vt_ref, o_ref, m_ref, l_ref, acc_ref, sa_ref, sb_ref):
    i = pl.program_id(2)
    TQ = TK = T_MLA
    m_ref[...] = jnp.full(m_ref.shape, NEG, jnp.float32)
    l_ref[...] = jnp.zeros(l_ref.shape, jnp.float32)
    acc_ref[...] = jnp.zeros(acc_ref.shape, jnp.float32)

    def scores(hh, j):
        start = pl.multiple_of(j * TK, TK)
        return lax.dot_general(k_ref[hh, pl.ds(start, TK), :], q_ref[hh], _DN_T,
                               preferred_element_type=jnp.float32)

    def step(j, src_ref, dst_ref, mask):
        for hh in range(HP_MLA):
            st = src_ref[hh]
            if mask is not None:
                st = jnp.where(mask, st, NEG)
            m_old = m_ref[hh]
            m_new = jnp.maximum(m_old, jnp.max(st, axis=0, keepdims=True))
            alpha = jnp.exp2(m_old - m_new)
            pt = jnp.exp2(st - m_new)
            l_ref[hh] = alpha * l_ref[hh] + jnp.sum(pt, axis=0, keepdims=True)
            if dst_ref is not None:
                dst_ref[hh] = scores(hh, j + 1)
            vt = vt_ref[j, hh * MLA_V:(hh + 1) * MLA_V, :]
            acc_ref[hh] = alpha * acc_ref[hh] + jnp.dot(vt, pt.astype(jnp.bfloat16),
                                                        preferred_element_type=jnp.float32)
            m_ref[hh] = m_new

    def pair(n, c):
        step(2 * n, sa_ref, sb_ref, None)
        step(2 * n + 1, sb_ref, sa_ref, None)
        return c

    r = lax.broadcasted_iota(jnp.int32, (TK, TQ), 0) // CHUNK
    c = lax.broadcasted_iota(jnp.int32, (TK, TQ), 1) // CHUNK
    diag = r <= c
    for hh in range(HP_MLA):
        sa_ref[hh] = scores(hh, 0)
    lax.fori_loop(0, i // 2, pair, 0)

    @pl.when(i % 2 == 0)
    def _():
        step(i, sa_ref, None, diag)

    @pl.when(i % 2 == 1)
    def _():
        step(i - 1, sa_ref, sb_ref, None)
        step(i, sb_ref, None, diag)

    for hh in range(HP_MLA):
        o_ref[:, hh * MLA_V:(hh + 1) * MLA_V] = (acc_ref[hh] / l_ref[hh]).T


def _mla_prompt(qm, km, vmt):
    H, B, S, E = qm.shape
    TQ = TK = T_MLA
    HP = HP_MLA
    assert S % TQ == 0 and H % HP == 0 and vmt.shape == (B, S // TK, H * MLA_V, TK)
    return pl.pallas_call(
        _mla_prompt_kernel,
        grid=(B, H // HP, S // TQ),
        in_specs=[pl.BlockSpec((HP, None, TQ, E), lambda b, h, i: (h, b, i, 0)),
                  pl.BlockSpec((HP, None, S, E), lambda b, h, i: (h, b, 0, 0)),
                  pl.BlockSpec((None, S // TK, HP * MLA_V, TK), lambda b, h, i: (b, 0, h, 0))],
        out_specs=pl.BlockSpec((None, TQ, HP * MLA_V), lambda b, h, i: (b, i, h)),
        out_shape=jax.ShapeDtypeStruct((B, S, H * MLA_V), jnp.float32),
        scratch_shapes=[pltpu.VMEM((HP, 1, TQ), jnp.float32), pltpu.VMEM((HP, 1, TQ), jnp.float32),
                        pltpu.VMEM((HP, MLA_V, TQ), jnp.float32),
                        pltpu.VMEM((HP, TK, TQ), jnp.float32), pltpu.VMEM((HP, TK, TQ), jnp.float32)],
        compiler_params=_params(("arbitrary", "arbitrary", "arbitrary")),
        name="mla_prompt",
    )(qm, km, vmt)


def _mla_sample_kernel(q_ref, kn_ref, vn_ref, ckv_ref, ckr_ref, wukv_ref, o_ref, kv_ref):
    bf = jnp.bfloat16
    P = ckv_ref.shape[0]
    step = min(512, P)
    for r in range(0, P, step):
        kv_ref[r:r + step, :] = jnp.dot(ckv_ref[r:r + step, :].astype(bf), wukv_ref[...],
                                        preferred_element_type=jnp.float32).astype(bf)
    ckr = ckr_ref[...].astype(bf)
    nk = N_HEADS * MLA_NOPE
    for h in range(N_HEADS):
        q = q_ref[h]
        s_c = (lax.dot_general(q[:, :MLA_NOPE], kv_ref[:, h * MLA_NOPE:(h + 1) * MLA_NOPE], _DN_T,
                               preferred_element_type=jnp.float32)
               + lax.dot_general(q[:, MLA_NOPE:MLA_NOPE + MLA_ROPE], ckr, _DN_T,
                                 preferred_element_type=jnp.float32))
        s_n = lax.dot_general(q, kn_ref[h], _DN_T, preferred_element_type=jnp.float32)
        m = jnp.maximum(jnp.max(s_c, axis=-1, keepdims=True), jnp.max(s_n, axis=-1, keepdims=True))
        p_c = jnp.exp2(s_c - m)
        p_n = jnp.exp2(s_n - m)
        den = jnp.sum(p_c, axis=-1, keepdims=True) + jnp.sum(p_n, axis=-1, keepdims=True)
        o = (jnp.dot(p_c.astype(bf), kv_ref[:, nk + h * MLA_V:nk + (h + 1) * MLA_V],
                     preferred_element_type=jnp.float32)
             + jnp.dot(p_n.astype(bf), vn_ref[:, h * MLA_V:(h + 1) * MLA_V],
                       preferred_element_type=jnp.float32))
        o_ref[:, h * MLA_V:(h + 1) * MLA_V] = o / den


def _mla_sample(qm, km, vm, cache_ckv, cache_kr, layer, w_ukv):
    H, B, T, E = qm.shape
    P = cache_ckv.shape[2]
    heads = pl.BlockSpec((H, None, T, E), lambda b: (0, b, 0, 0))
    return pl.pallas_call(
        _mla_sample_kernel,
        grid=(B,),
        in_specs=[heads, heads,
                  pl.BlockSpec((None, T, H * MLA_V), lambda b: (b, 0, 0)),
                  pl.BlockSpec((None, None, P, MLA_RANK), lambda b: (layer, b, 0, 0)),
                  pl.BlockSpec((None, None, P, MLA_ROPE), lambda b: (layer, b, 0, 0)),
                  pl.BlockSpec(w_ukv.shape, lambda b: (0, 0))],
        out_specs=pl.BlockSpec((None, T, H * MLA_V), lambda b: (b, 0, 0)),
        out_shape=jax.ShapeDtypeStruct((B, T, H * MLA_V), jnp.float32),
        scratch_shapes=[pltpu.VMEM((P, w_ukv.shape[1]), jnp.bfloat16)],
        compiler_params=_params(("arbitrary",)),
        name="mla_sample",
    )(qm, km, vm, cache_ckv, cache_kr, w_ukv)


def _neg_tri(n):
    r = lax.broadcasted_iota(jnp.int32, (n, n), 0)
    c = lax.broadcasted_iota(jnp.int32, (n, n), 1)
    return jnp.where(r >= c, -1.0, 0.0).astype(jnp.bfloat16)


def _sb_init(q, qh_ref, acc_ref, car_ref):
    for h in range(N_HEADS):
        qh_ref[h] = jnp.where(_head_mask(HEADS_W, h), q, jnp.zeros_like(q))
    acc_ref[...] = jnp.zeros(acc_ref.shape, jnp.float32)
    car_ref[...] = jnp.zeros(car_ref.shape, jnp.float32)


def _sb_tile(qh_ref, k, v, acc_ref, car_ref, ntri, mask):
    def scores(h):
        return lax.dot_general(qh_ref[h], k, _DN_T, preferred_element_type=jnp.float32)

    def cumsum(z):
        neg_abs = lax.bitcast_convert_type(lax.bitcast_convert_type(z, jnp.uint32) | jnp.uint32(0x80000000),
                                           jnp.float32)
        sp = jnp.maximum(z, 0.0) + jnp.log2(1.0 + jnp.exp2(neg_abs))
        if mask is not None:
            sp = jnp.where(mask, sp, 0.0)
        return jnp.dot(sp.astype(jnp.bfloat16), ntri, preferred_element_type=jnp.float32)

    def accumulate(h, z, inner):
        a = jnp.minimum(jnp.exp2(z + inner + car_ref[h]), 1.0)
        if mask is not None:
            a = jnp.where(mask, a, 0.0)
        acc_ref[h] += jnp.dot(a.astype(jnp.bfloat16), v, preferred_element_type=jnp.float32)
        car_ref[h] += inner[:, 0:1]

    zs, inners = {}, {}
    zs[0] = scores(0)
    for t in range(1, N_HEADS + 2):
        if t < N_HEADS:
            zs[t] = scores(t)
        if 0 <= t - 1 < N_HEADS:
            inners[t - 1] = cumsum(zs[t - 1])
        if 0 <= t - 2 < N_HEADS:
            accumulate(t - 2, zs.pop(t - 2), inners.pop(t - 2))


def _sb_finish(acc_ref):
    return functools.reduce(jnp.add, [jnp.where(_head_mask(HEADS_W, h), acc_ref[h], 0.0)
                                      for h in range(N_HEADS)])


def _sb_prompt_kernel(q_ref, k_ref, v_ref, o_ref, qh_ref, acc_ref, car_ref):
    i = pl.program_id(1)
    TQ, TK = TQ_SB, TK_SB
    nd = TQ // TK
    _sb_init(q_ref[...], qh_ref, acc_ref, car_ref)
    ntri = _neg_tri(TK)
    row = i * TQ + lax.broadcasted_iota(jnp.int32, (TQ, TK), 0)
    col = lax.broadcasted_iota(jnp.int32, (TQ, TK), 1)
    for d in range(nd - 1, -1, -1):
        start = pl.multiple_of((i * nd + d) * TK, TK)
        _sb_tile(qh_ref, k_ref[pl.ds(start, TK), :], v_ref[pl.ds(start, TK), :], acc_ref, car_ref, ntri,
                 start + col < row)

    def body(n, c):
        start = pl.multiple_of((i * nd - 1 - n) * TK, TK)
        _sb_tile(qh_ref, k_ref[pl.ds(start, TK), :], v_ref[pl.ds(start, TK), :], acc_ref, car_ref, ntri, None)
        return c

    lax.fori_loop(0, i * nd, body, 0)
    o_ref[...] = _sb_finish(acc_ref)


def _sb_prompt(q, k, v):
    B, S, W = q.shape
    TQ = TQ_SB
    assert S % TQ == 0 and TQ % TK_SB == 0
    return pl.pallas_call(
        _sb_prompt_kernel,
        grid=(B, S // TQ),
        in_specs=[pl.BlockSpec((None, TQ, W), lambda b, i: (b, i, 0)),
                  pl.BlockSpec((None, S, W), lambda b, i: (b, 0, 0)),
                  pl.BlockSpec((None, S, W), lambda b, i: (b, 0, 0))],
        out_specs=pl.BlockSpec((None, TQ, W), lambda b, i: (b, i, 0)),
        out_shape=jax.ShapeDtypeStruct((B, S, W), jnp.float32),
        scratch_shapes=[pltpu.VMEM((N_HEADS, TQ, W), jnp.bfloat16), pltpu.VMEM((N_HEADS, TQ, W), jnp.float32),
                        pltpu.VMEM((N_HEADS, TQ, 1), jnp.float32)],
        compiler_params=_params(("arbitrary", "arbitrary")),
        name="sb_prompt",
    )(q, k, v)


def _sb_sample_kernel(q_ref, kn_ref, vn_ref, ck_ref, cv_ref, o_ref, kb_ref, vb_ref, qh_ref, acc_ref, car_ref):
    bf = jnp.bfloat16
    T = q_ref.shape[0]
    P = ck_ref.shape[0]
    TK = min(TK_SB, P)
    kb_ref[...] = ck_ref[...].astype(bf)
    vb_ref[...] = cv_ref[...].astype(bf)
    _sb_init(q_ref[...], qh_ref, acc_ref, car_ref)
    r = lax.broadcasted_iota(jnp.int32, (T, T), 0)
    c = lax.broadcasted_iota(jnp.int32, (T, T), 1)
    _sb_tile(qh_ref, kn_ref[...], vn_ref[...], acc_ref, car_ref, _neg_tri(T), c < r)
    ntri = _neg_tri(TK)

    def body(n, carry):
        start = pl.multiple_of(P - (n + 1) * TK, TK)
        _sb_tile(qh_ref, kb_ref[pl.ds(start, TK), :], vb_ref[pl.ds(start, TK), :], acc_ref, car_ref, ntri, None)
        return carry

    lax.fori_loop(0, P // TK, body, 0)
    o_ref[...] = _sb_finish(acc_ref)


def _sb_sample(q, k_new, v_new, cache_k, cache_v, layer):
    B, T, W = q.shape
    P = cache_k.shape[2]
    assert P % min(TK_SB, P) == 0
    new = pl.BlockSpec((None, T, W), lambda b: (b, 0, 0))
    cache = pl.BlockSpec((None, None, P, W), lambda b: (layer, b, 0, 0))
    return pl.pallas_call(
        _sb_sample_kernel,
        grid=(B,),
        in_specs=[new, new, new, cache, cache],
        out_specs=new,
        out_shape=jax.ShapeDtypeStruct((B, T, W), jnp.float32),
        scratch_shapes=[pltpu.VMEM((P, W), jnp.bfloat16), pltpu.VMEM((P, W), jnp.bfloat16),
                        pltpu.VMEM((N_HEADS, T, W), jnp.bfloat16), pltpu.VMEM((N_HEADS, T, W), jnp.float32),
                        pltpu.VMEM((N_HEADS, T, 1), jnp.float32)],
        compiler_params=_params(("arbitrary",)),
        name="sb_sample",
    )(q, k_new, v_new, cache_k, cache_v)


def _merge_ffn_kernel(final, x_ref, oa_ref, om_ref, os_ref, goa_ref, gom_ref, gos_ref, wout_ref,
                      gffn_ref, wup_ref, wdown_ref, gfin_ref, y_ref, x1_ref, h_ref, acc_ref):
    bf = jnp.bfloat16
    j = pl.program_id(1)

    @pl.when(j == 0)
    def _():
        cat = jnp.concatenate([_rms(oa_ref[...], goa_ref[...]).astype(bf),
                               _rms(om_ref[...], gom_ref[...]).astype(bf),
                               _rms(os_ref[...], gos_ref[...]).astype(bf)], axis=-1)
        x1 = x_ref[...] + jnp.dot(cat, wout_ref[...], preferred_element_type=jnp.float32)
        x1_ref[...] = x1
        h_ref[...] = _rms(x1, gffn_ref[...]).astype(bf)
        acc_ref[...] = jnp.zeros(acc_ref.shape, jnp.float32)

    u = jnp.maximum(jnp.dot(h_ref[...], wup_ref[...], preferred_element_type=jnp.float32), 0.0)
    acc_ref[...] += jnp.dot((u * u).astype(bf), wdown_ref[...], preferred_element_type=jnp.float32)

    @pl.when(j == pl.num_programs(1) - 1)
    def _():
        y = x1_ref[...] + acc_ref[...]
        y_ref[...] = _rms(y, gfin_ref[...]) if final else y


def _merge_ffn(x, oa, om, osb, g_oa, g_om, g_os, w_out, g_ffn, w_up, w_down, g_final, final):
    N, D = x.shape
    F = w_up.shape[1]
    tm = min(TM_FFN, N)
    tf = min(TF_FFN, F)
    assert N % tm == 0 and F % tf == 0
    row = lambda w: pl.BlockSpec((tm, w), lambda i, j: (i, 0))
    full = lambda a: pl.BlockSpec(a.shape, lambda i, j: (0,) * a.ndim)
    return pl.pallas_call(
        functools.partial(_merge_ffn_kernel, final),
        grid=(N // tm, F // tf),
        in_specs=[row(D), row(oa.shape[1]), row(om.shape[1]), row(osb.shape[1]),
                  full(g_oa), full(g_om), full(g_os), full(w_out), full(g_ffn),
                  pl.BlockSpec((D, tf), lambda i, j: (0, j)),
                  pl.BlockSpec((tf, D), lambda i, j: (j, 0)),
                  full(g_final)],
        out_specs=row(D),
        out_shape=jax.ShapeDtypeStruct((N, D), jnp.float32),
        scratch_shapes=[pltpu.VMEM((tm, D), jnp.float32), pltpu.VMEM((tm, D), jnp.bfloat16),
                        pltpu.VMEM((tm, D), jnp.float32)],
        compiler_params=_params(("arbitrary", "arbitrary")),
        name="merge_ffn",
    )(x, oa, om, osb, g_oa, g_om, g_os, w_out, g_ffn, w_up, w_down, g_final)


def _swap_halves(w):
    half = w.shape[-1] // 2
    return jnp.concatenate([w[..., half:], w[..., :half]], axis=-1)


def _pad_cols(w, width):
    return jnp.pad(w, [(0, 0)] * (w.ndim - 1) + [(0, width - w.shape[-1])])


def _prep_layer(w_in, w_uq, w_ukv):
    bf = jnp.bfloat16
    W = HEADS_W
    a_end = 3 * W
    cq_end = a_end + MLA_RANK
    ckv_end = cq_end + MLA_RANK
    kr_end = ckv_end + MLA_ROPE
    w_kr = w_in[:, ckv_end:kr_end]
    w_in_p = jnp.concatenate([w_in[:, :ckv_end], w_in[:, kr_end:],
                              _pad_cols(w_kr, ROPE_PAD), _pad_cols(_swap_halves(w_kr), ROPE_PAD)], axis=1)
    qn = w_uq[:, :, :MLA_NOPE].reshape(MLA_RANK, N_HEADS * MLA_NOPE)
    qp = w_uq[:, :, MLA_NOPE:]
    w_uq_p = jnp.concatenate([qn, _pad_cols(qp, ROPE_PAD).reshape(MLA_RANK, -1),
                              _pad_cols(_swap_halves(qp), ROPE_PAD).reshape(MLA_RANK, -1)], axis=1)
    w_ukv_p = jnp.concatenate([w_ukv[:, :, :MLA_NOPE].reshape(MLA_RANK, -1),
                               w_ukv[:, :, MLA_NOPE:].reshape(MLA_RANK, -1)], axis=1)
    return w_in_p.astype(bf), w_uq_p.astype(bf), w_ukv_p.astype(bf)


def _rope_tables(pos):
    half = MLA_ROPE // 2
    inv = ROPE_THETA ** (-jnp.arange(half, dtype=jnp.float32) / half)
    ang = pos.astype(jnp.float32)[:, None] * inv[None, :]
    cos, sin = jnp.cos(ang), jnp.sin(ang)
    return (_pad_cols(jnp.concatenate([cos, cos], axis=-1), ROPE_PAD),
            _pad_cols(jnp.concatenate([-sin, sin], axis=-1), ROPE_PAD))


def _band_bias_chunk(rel):
    R = rel.shape[-1]
    assert R == CHUNK + REL_MAX
    K = BAND + CHUNK
    L = K + CHUNK - 1
    lead = rel.shape[:-1]
    g = jnp.concatenate([rel, jnp.broadcast_to(rel[..., R - 1:], lead + (L - R,))], axis=-1)
    g = _pad_cols(jnp.flip(g, axis=-1), L + 1)
    m = jnp.tile(g, (1,) * len(lead) + (CHUNK,))[..., :CHUNK * L].reshape(lead + (CHUNK, L))
    return m[..., CHUNK - 1:CHUNK - 1 + K]


def _band_bias_prompt(chunk_bias):
    n = TQ_BAND // CHUNK
    lead = [(0, 0)] * (chunk_bias.ndim - 1)
    rows = [jnp.pad(chunk_bias, lead + [(CHUNK * c, CHUNK * (n - 1 - c))], constant_values=NEG) for c in range(n)]
    return jnp.concatenate(rows, axis=-2)


def kernel(x_prompt, x_sample, cache_a_k, cache_a_v, cache_mla_ckv, cache_mla_krope, cache_sb_k, cache_sb_v,
           g_mix, w_in, g_cq, g_ckv, w_uq, w_ukv, a_rel_bias, g_out_a, g_out_mla, g_out_sb, w_out,
           g_ffn, w_up, w_down, g_final):
    bf = jnp.bfloat16
    B, S, D = x_prompt.shape
    DB, T, _ = x_sample.shape
    depth = w_in.shape[0]
    LA = cache_a_k.shape[2]
    P = cache_mla_ckv.shape[2]
    lc = min(BAND, S)
    assert T == CHUNK and P % CHUNK == 0 and LA == BAND and lc == BAND

    cache_a_k = cache_a_k.reshape(depth, DB, LA, HEADS_W)
    cache_a_v = cache_a_v.reshape(depth, DB, LA, HEADS_W)
    cache_sb_k = cache_sb_k.reshape(depth, DB, P, HEADS_W)
    cache_sb_v = cache_sb_v.reshape(depth, DB, P, HEADS_W)

    tm_s = min(TM_IN, DB * T)
    cos_p, sin_p = _rope_tables(jnp.arange(S))
    cos_s, sin_s = _rope_tables(P + jnp.arange(tm_s) % T)
    row = lambda g: g.reshape(1, -1)
    chunk_bias = _band_bias_chunk(a_rel_bias)
    bias_p = _band_bias_prompt(chunk_bias)

    xp = x_prompt.reshape(B * S, D)
    xs = x_sample.reshape(DB * T, D)
    p_states, s_states = [], []
    for l in range(depth):
        w_in_p, w_uq_p, w_ukv_p = _prep_layer(w_in[l], w_uq[l], w_ukv[l])
        w_out_b, w_up_b, w_down_b = w_out[l].astype(bf), w_up[l].astype(bf), w_down[l].astype(bf)
        last = l == depth - 1
        lw_in = (row(g_mix[l]), w_in_p, row(g_cq[l]), row(g_ckv[l]), w_uq_p, w_ukv_p)
        lw_out = (row(g_out_a[l]), row(g_out_mla[l]), row(g_out_sb[l]), w_out_b, row(g_ffn[l]),
                  w_up_b, w_down_b, row(g_final), last)

        (qa, ka, va, kaf, vaf, ckv, kr, qc, kc, vc, kcf, vcf, qm, km, _, vmt) = _in_proj(
            xp, *lw_in, cos_p, sin_p, S, lc)
        seq = lambda a: a.reshape(B, S, a.shape[-1])
        oa = _band_prompt(seq(qa), seq(ka), seq(va), bias_p[l])
        om = _mla_prompt(qm.reshape(N_HEADS, B, S, MLA_QK), km.reshape(N_HEADS, B, S, MLA_QK),
                         vmt.reshape(B, S // T_MLA, N_HEADS * MLA_V, T_MLA))
        osb = _sb_prompt(seq(qc), seq(kc), seq(vc))
        xp = _merge_ffn(xp, oa.reshape(B * S, -1), om.reshape(B * S, -1), osb.reshape(B * S, -1), *lw_out)
        p_states.append((kaf.reshape(B, lc, N_HEADS, HEAD_DIM), vaf.reshape(B, lc, N_HEADS, HEAD_DIM),
                         ckv.reshape(B, S, MLA_RANK), kr.reshape(B, S, MLA_ROPE),
                         kcf.reshape(B, S, N_HEADS, HEAD_DIM), vcf.reshape(B, S, N_HEADS, HEAD_DIM)))

        (qa, ka, va, kaf, vaf, ckv, kr, qc, kc, vc, kcf, vcf, qm, km, vm, _) = _in_proj(
            xs, *lw_in, cos_s, sin_s, T, T)
        seq = lambda a: a.reshape(DB, T, a.shape[-1])
        bias_c, bias_n = chunk_bias[l, :, :, :LA], chunk_bias[l, :, :, LA:]
        oa, sk, sv = _band_sample(seq(qa), seq(kaf), seq(vaf), cache_a_k, cache_a_v, l, bias_c, bias_n)
        om = _mla_sample(qm.reshape(N_HEADS, DB, T, MLA_QK), km.reshape(N_HEADS, DB, T, MLA_QK), seq(vm),
                         cache_mla_ckv, cache_mla_krope, l, w_ukv_p)
        osb = _sb_sample(seq(qc), seq(kc), seq(vc), cache_sb_k, cache_sb_v, l)
        xs = _merge_ffn(xs, oa.reshape(DB * T, -1), om.reshape(DB * T, -1), osb.reshape(DB * T, -1), *lw_out)
        s_states.append((sk.reshape(DB, LA, N_HEADS, HEAD_DIM), sv.reshape(DB, LA, N_HEADS, HEAD_DIM),
                         ckv.reshape(DB, T, MLA_RANK), kr.reshape(DB, T, MLA_ROPE),
                         kcf.reshape(DB, T, N_HEADS, HEAD_DIM), vcf.reshape(DB, T, N_HEADS, HEAD_DIM)))

    p_out = [jnp.stack(t, axis=0) for t in zip(*p_states)]
    s_out = [jnp.stack(t, axis=0) for t in zip(*s_states)]
    return (xp.reshape(B, S, D), xs.reshape(DB, T, D), *p_out, *s_out)
```

```python
import functools
import math

import numpy as np
import jax
import jax.numpy as jnp
from jax import lax
from jax.experimental import pallas as pl
from jax.experimental.pallas import tpu as pltpu

CHUNK = 64
HEAD_DIM = 64
N_HEADS = 4
BAND = 8 * CHUNK
REL_MAX = 128
MLA_NOPE = 128
MLA_ROPE = 64
MLA_V = 128
MLA_RANK = 256
ROPE_THETA = 10000.0
EPS = 1e-6
NEG = -1e30

HEADS_W = N_HEADS * HEAD_DIM
MLA_QK = 256
ROPE_PAD = 128

TQ_BAND = 256
TM_IN = 512
T_MLA = TM_IN
HP_MLA = 2
TQ_SB = 512
TK_SB = 256
LOG2E = math.log2(math.e)
TM_FFN = 512
TF_FFN = 1024
VMEM_LIMIT = 56 * 1024 * 1024

_DN_T = (((1,), (1,)), ((), ()))


def _params(sem, flags=None):
    return pltpu.CompilerParams(dimension_semantics=sem, vmem_limit_bytes=VMEM_LIMIT, flags=flags)


def _rms(x, g):
    return x * lax.rsqrt(jnp.mean(x * x, axis=-1, keepdims=True) + EPS) * g


def _head_mask(width, h):
    lane = lax.broadcasted_iota(jnp.int32, (1, width), 1)
    return (lane // HEAD_DIM) == h


def _in_proj_kernel(x_ref, g_ref, w_ref, gcq_ref, gckv_ref, wuq_ref, wukv_ref, cos_ref, sin_ref,
                    qa_ref, ka_ref, va_ref, kaf_ref, vaf_ref, ckv_ref, kr_ref,
                    qc_ref, kc_ref, vc_ref, kcf_ref, vcf_ref, qm_ref, km_ref, vm_ref, vmt_ref):
    bf = jnp.bfloat16
    h = _rms(x_ref[...], g_ref[...]).astype(bf)
    p = jnp.dot(h, w_ref[...], preferred_element_type=jnp.float32)
    W = HEADS_W
    qa_ref[...] = (p[:, 0:W] * HEAD_DIM ** -0.5).astype(bf)
    ka = p[:, W:2 * W]
    va = p[:, 2 * W:3 * W]
    ka_ref[...] = ka.astype(bf)
    va_ref[...] = va.astype(bf)
    kaf_ref[...] = ka
    vaf_ref[...] = va
    cq = _rms(p[:, 3 * W:4 * W], gcq_ref[...])
    ckv = _rms(p[:, 4 * W:5 * W], gckv_ref[...])
    ckv_ref[...] = ckv
    qc_ref[...] = (p[:, 5 * W:6 * W] * (HEAD_DIM ** -0.5 * LOG2E)).astype(bf)
    kc = p[:, 6 * W:7 * W]
    vc = p[:, 7 * W:8 * W]
    kc_ref[...] = kc.astype(bf)
    vc_ref[...] = vc.astype(bf)
    kcf_ref[...] = kc
    vcf_ref[...] = vc
    cos = cos_ref[...]
    sin = sin_ref[...]
    kr = p[:, 8 * W:8 * W + ROPE_PAD] * cos + p[:, 8 * W + ROPE_PAD:8 * W + 2 * ROPE_PAD] * sin
    kr_ref[...] = kr[:, :MLA_ROPE]
    krb = kr.astype(bf)
    q = jnp.dot(cq.astype(bf), wuq_ref[...], preferred_element_type=jnp.float32)
    kv = jnp.dot(ckv.astype(bf), wukv_ref[...], preferred_element_type=jnp.float32)
    scale = (MLA_NOPE + MLA_ROPE) ** -0.5 * LOG2E
    nq = N_HEADS * MLA_NOPE
    for hh in range(N_HEADS):
        qn = q[:, hh * MLA_NOPE:(hh + 1) * MLA_NOPE]
        qp = q[:, nq + hh * ROPE_PAD:nq + (hh + 1) * ROPE_PAD]
        qs = q[:, nq + (N_HEADS + hh) * ROPE_PAD:nq + (N_HEADS + hh + 1) * ROPE_PAD]
        qm_ref[hh, :, 0:MLA_NOPE] = (qn * scale).astype(bf)
        qm_ref[hh, :, MLA_NOPE:MLA_QK] = ((qp * cos + qs * sin) * scale).astype(bf)
        km_ref[hh, :, 0:MLA_NOPE] = kv[:, hh * MLA_NOPE:(hh + 1) * MLA_NOPE].astype(bf)
        km_ref[hh, :, MLA_NOPE:MLA_QK] = krb
    vm_ref[...] = kv[:, nq:].astype(bf)
    vmt_ref[...] = kv[:, nq:].T.astype(bf)


def _in_proj(x, g, w_in, g_cq, g_ckv, w_uq, w_ukv, cos, sin, rows_per_seq, tail_rows):
    N, D = x.shape
    tm = min(TM_IN, N)
    assert N % tm == 0 and cos.shape[0] % tm == 0
    n_tab = cos.shape[0] // tm
    n_seq = N // rows_per_seq
    if tail_rows == rows_per_seq:
        tail_map = lambda i: (i, 0)
    else:
        assert tail_rows % tm == 0 and rows_per_seq % tm == 0
        tiles_per_seq = rows_per_seq // tm
        tail_tiles = tail_rows // tm

        def tail_map(i):
            b = i // tiles_per_seq
            t = i % tiles_per_seq
            return (b * tail_tiles + jnp.maximum(t - (tiles_per_seq - tail_tiles), 0), 0)

    row = lambda w: pl.BlockSpec((tm, w), lambda i: (i, 0))
    full = lambda a: pl.BlockSpec(a.shape, lambda i: (0,) * a.ndim)
    tab = pl.BlockSpec((tm, ROPE_PAD), lambda i: (i % n_tab, 0))
    tail = pl.BlockSpec((tm, HEADS_W), tail_map)
    heads = pl.BlockSpec((N_HEADS, tm, MLA_QK), lambda i: (0, i, 0))
    bf, f32 = jnp.bfloat16, jnp.float32
    sds = jax.ShapeDtypeStruct
    out_shape = [
        sds((N, HEADS_W), bf), sds((N, HEADS_W), bf), sds((N, HEADS_W), bf),
        sds((n_seq * tail_rows, HEADS_W), f32), sds((n_seq * tail_rows, HEADS_W), f32),
        sds((N, MLA_RANK), f32), sds((N, MLA_ROPE), f32),
        sds((N, HEADS_W), bf), sds((N, HEADS_W), bf), sds((N, HEADS_W), bf),
        sds((N, HEADS_W), f32), sds((N, HEADS_W), f32),
        sds((N_HEADS, N, MLA_QK), bf), sds((N_HEADS, N, MLA_QK), bf),
        sds((N, N_HEADS * MLA_V), bf),
        sds((N // tm, N_HEADS * MLA_V, tm), bf),
    ]
    out_specs = [row(HEADS_W), row(HEADS_W), row(HEADS_W), tail, tail,
                 row(MLA_RANK), row(MLA_ROPE),
                 row(HEADS_W), row(HEADS_W), row(HEADS_W), row(HEADS_W), row(HEADS_W),
                 heads, heads, row(N_HEADS * MLA_V),
                 pl.BlockSpec((None, N_HEADS * MLA_V, tm), lambda i: (i, 0, 0))]
    return pl.pallas_call(
        _in_proj_kernel,
        grid=(N // tm,),
        in_specs=[row(D), full(g), full(w_in), full(g_cq), full(g_ckv), full(w_uq), full(w_ukv), tab, tab],
        out_specs=out_specs,
        out_shape=out_shape,
        compiler_params=_params(("arbitrary",)),
        name="in_proj",
    )(x, g, w_in, g_cq, g_ckv, w_uq, w_ukv, cos, sin)


def _band_heads(q, kparts, vparts, biases, valid):
    out = jnp.zeros((q.shape[0], HEADS_W), jnp.float32)
    hms = [_head_mask(HEADS_W, h) for h in range(N_HEADS)]
    raw = [[lax.dot_general(jnp.where(hm, q, jnp.zeros_like(q)), k, _DN_T, preferred_element_type=jnp.float32)
            for k in kparts] for hm in hms]
    for h in range(N_HEADS):
        hm = hms[h]
        ss = []
        for i in range(len(kparts)):
            s = raw[h][i] + biases[i][h]
            if valid is not None and valid[i] is not None:
                s = jnp.where(valid[i], s, NEG)
            ss.append(s)
        m = functools.reduce(jnp.maximum, [jnp.max(s, axis=-1, keepdims=True) for s in ss])
        ps = [jnp.exp(s - m) for s in ss]
        den = functools.reduce(jnp.add, [jnp.sum(p, axis=-1, keepdims=True) for p in ps])
        o = functools.reduce(jnp.add, [jnp.dot(p.astype(jnp.bfloat16), v, preferred_element_type=jnp.float32)
                                       for p, v in zip(ps, vparts)])
        out = out + jnp.where(hm, o / den, 0.0)
    return out


def _band_prompt_kernel(q_ref, k_ref, v_ref, bias_ref, o_ref):
    i = pl.program_id(1)
    T = TQ_BAND
    nprev = BAND // T
    starts = [pl.multiple_of(jnp.maximum(i - (nprev - j), 0) * T, T) for j in range(nprev + 1)]
    k = jnp.concatenate([k_ref[pl.ds(s, T), :] for s in starts], axis=0)
    v = jnp.concatenate([v_ref[pl.ds(s, T), :] for s in starts], axis=0)
    col = lax.broadcasted_iota(jnp.int32, (1, BAND + T), 1)
    valid = col >= BAND - i * T
    o_ref[...] = _band_heads(q_ref[...], [k], [v], [bias_ref], [valid])


def _band_prompt(q, k, v, bias):
    B, S, W = q.shape
    T = TQ_BAND
    assert S % T == 0 and BAND % T == 0
    return pl.pallas_call(
        _band_prompt_kernel,
        grid=(B, S // T),
        in_specs=[pl.BlockSpec((None, T, W), lambda b, i: (b, i, 0)),
                  pl.BlockSpec((None, S, W), lambda b, i: (b, 0, 0)),
                  pl.BlockSpec((None, S, W), lambda b, i: (b, 0, 0)),
                  pl.BlockSpec(bias.shape, lambda b, i: (0, 0, 0))],
        out_specs=pl.BlockSpec((None, T, W), lambda b, i: (b, i, 0)),
        out_shape=jax.ShapeDtypeStruct((B, S, W), jnp.float32),
        compiler_params=_params(("arbitrary", "arbitrary")),
        name="band_prompt",
    )(q, k, v, bias)


def _band_sample_kernel(q_ref, kn_ref, vn_ref, ck_ref, cv_ref, bc_ref, bn_ref, o_ref, sk_ref, sv_ref):
    bf = jnp.bfloat16
    ck, cv, kn, vn = ck_ref[...], cv_ref[...], kn_ref[...], vn_ref[...]
    o_ref[...] = _band_heads(q_ref[...], [ck.astype(bf), kn.astype(bf)], [cv.astype(bf), vn.astype(bf)],
                             [bc_ref, bn_ref], None)
    n_keep = ck.shape[0] - kn.shape[0]
    sk_ref[0:n_keep, :] = ck[kn.shape[0]:, :]
    sk_ref[n_keep:, :] = kn
    sv_ref[0:n_keep, :] = cv[vn.shape[0]:, :]
    sv_ref[n_keep:, :] = vn


def _band_sample(q, k_new, v_new, cache_k, cache_v, layer, bias_c, bias_n):
    B, T, W = q.shape
    LA = cache_k.shape[2]
    new = pl.BlockSpec((None, T, W), lambda b: (b, 0, 0))
    cache = pl.BlockSpec((None, None, LA, W), lambda b: (layer, b, 0, 0))
    roll = pl.BlockSpec((None, LA, W), lambda b: (b, 0, 0))
    full = lambda a: pl.BlockSpec(a.shape, lambda b: (0,) * a.ndim)
    return pl.pallas_call(
        _band_sample_kernel,
        grid=(B,),
        in_specs=[new, new, new, cache, cache, full(bias_c), full(bias_n)],
        out_specs=[new, roll, roll],
        out_shape=[jax.ShapeDtypeStruct((B, T, W), jnp.float32),
                   jax.ShapeDtypeStruct((B, LA, W), jnp.float32),
                   jax.ShapeDtypeStruct((B, LA, W), jnp.float32)],
        compiler_params=_params(("arbitrary",)),
        name="band_sample",
    )(q, k_new, v_new, cache_k, cache_v, bias_c, bias_n)


def _mla_prompt_kernel(q_ref, k_ref, vt_ref, o_ref, m_ref, l_ref, acc_ref, sa_ref, sb_ref):
    i = pl.program_id(2)
    TQ = TK = T_MLA
    m_ref[...] = jnp.full(m_ref.shape, NEG, jnp.float32)
    l_ref[...] = jnp.zeros(l_ref.shape, jnp.float32)
    acc_ref[...] = jnp.zeros(acc_ref.shape, jnp.float32)

    def scores(hh, j):
        start = pl.multiple_of(j * TK, TK)
        return lax.dot_general(k_ref[hh, pl.ds(start, TK), :], q_ref[hh], _DN_T,
                               preferred_element_type=jnp.float32)

    def step(j, src_ref, dst_ref, mask):
        for hh in range(HP_MLA):
            st = src_ref[hh]
            if mask is not None:
                st = jnp.where(mask, st, NEG)
            m_old = m_ref[hh]
            m_new = jnp.maximum(m_old, jnp.max(st, axis=0, keepdims=True))
            alpha = jnp.exp2(m_old - m_new)
            pt = jnp.exp2(st - m_new)
            l_ref[hh] = alpha * l_ref[hh] + jnp.sum(pt, axis=0, keepdims=True)
            if dst_ref is not None:
                dst_ref[hh] = scores(hh, j + 1)
            vt = vt_ref[j, hh * MLA_V:(hh + 1) * MLA_V, :]
            acc_ref[hh] = alpha * acc_ref[hh] + jnp.dot(vt, pt.astype(jnp.bfloat16),
                                                        preferred_element_type=jnp.float32)
            m_ref[hh] = m_new

    def pair(n, c):
        step(2 * n, sa_ref, sb_ref, None)
        step(2 * n + 1, sb_ref, sa_ref, None)
        return c

    r = lax.broadcasted_iota(jnp.int32, (TK, TQ), 0) // CHUNK
    c = lax.broadcasted_iota(jnp.int32, (TK, TQ), 1) // CHUNK
    diag = r <= c
    for hh in range(HP_MLA):
        sa_ref[hh] = scores(hh, 0)
    lax.fori_loop(0, i // 2, pair, 0)

    @pl.when(i % 2 == 0)
    def _():
        step(i, sa_ref, None, diag)

    @pl.when(i % 2 == 1)
    def _():
        step(i - 1, sa_ref, sb_ref, None)
        step(i, sb_ref, None, diag)

    for hh in range(HP_MLA):
        o_ref[:, hh * MLA_V:(hh + 1) * MLA_V] = (acc_ref[hh] / l_ref[hh]).T


def _mla_prompt(qm, km, vmt):
    H, B, S, E = qm.shape
    TQ = TK = T_MLA
    HP = HP_MLA
    assert S % TQ == 0 and H % HP == 0 and vmt.shape == (B, S // TK, H * MLA_V, TK)
    return pl.pallas_call(
        _mla_prompt_kernel,
        grid=(B, H // HP, S // TQ),
        in_specs=[pl.BlockSpec((HP, None, TQ, E), lambda b, h, i: (h, b, i, 0)),
                  pl.BlockSpec((HP, None, S, E), lambda b, h, i: (h, b, 0, 0)),
                  pl.BlockSpec((None, S // TK, HP * MLA_V, TK), lambda b, h, i: (b, 0, h, 0))],
        out_specs=pl.BlockSpec((None, TQ, HP * MLA_V), lambda b, h, i: (b, i, h)),
        out_shape=jax.ShapeDtypeStruct((B, S, H * MLA_V), jnp.float32),
        scratch_shapes=[pltpu.VMEM((HP, 1, TQ), jnp.float32), pltpu.VMEM((HP, 1, TQ), jnp.float32),
                        pltpu.VMEM((HP, MLA_V, TQ), jnp.float32),
                        pltpu.VMEM((HP, TK, TQ), jnp.float32), pltpu.VMEM((HP, TK, TQ), jnp.float32)],
        compiler_params=_params(("arbitrary", "arbitrary", "arbitrary")),
        name="mla_prompt",
    )(qm, km, vmt)


def _mla_sample_kernel(q_ref, kn_ref, vn_ref, ckv_ref, ckr_ref, wukv_ref, o_ref, kv_ref):
    bf = jnp.bfloat16
    P = ckv_ref.shape[0]
    step = min(512, P)
    for r in range(0, P, step):
        kv_ref[r:r + step, :] = jnp.dot(ckv_ref[r:r + step, :].astype(bf), wukv_ref[...],
                                        preferred_element_type=jnp.float32).astype(bf)
    ckr = ckr_ref[...].astype(bf)
    nk = N_HEADS * MLA_NOPE
    for h in range(N_HEADS):
        q = q_ref[h]
        s_c = (lax.dot_general(q[:, :MLA_NOPE], kv_ref[:, h * MLA_NOPE:(h + 1) * MLA_NOPE], _DN_T,
                               preferred_element_type=jnp.float32)
               + lax.dot_general(q[:, MLA_NOPE:MLA_NOPE + MLA_ROPE], ckr, _DN_T,
                                 preferred_element_type=jnp.float32))
        s_n = lax.dot_general(q, kn_ref[h], _DN_T, preferred_element_type=jnp.float32)
        m = jnp.maximum(jnp.max(s_c, axis=-1, keepdims=True), jnp.max(s_n, axis=-1, keepdims=True))
        p_c = jnp.exp2(s_c - m)
        p_n = jnp.exp2(s_n - m)
        den = jnp.sum(p_c, axis=-1, keepdims=True) + jnp.sum(p_n, axis=-1, keepdims=True)
        o = (jnp.dot(p_c.astype(bf), kv_ref[:, nk + h * MLA_V:nk + (h + 1) * MLA_V],
                     preferred_element_type=jnp.float32)
             + jnp.dot(p_n.astype(bf), vn_ref[:, h * MLA_V:(h + 1) * MLA_V],
                       preferred_element_type=jnp.float32))
        o_ref[:, h * MLA_V:(h + 1) * MLA_V] = o / den


def _mla_sample(qm, km, vm, cache_ckv, cache_kr, layer, w_ukv):
    H, B, T, E = qm.shape
    P = cache_ckv.shape[2]
    heads = pl.BlockSpec((H, None, T, E), lambda b: (0, b, 0, 0))
    return pl.pallas_call(
        _mla_sample_kernel,
        grid=(B,),
        in_specs=[heads, heads,
                  pl.BlockSpec((None, T, H * MLA_V), lambda b: (b, 0, 0)),
                  pl.BlockSpec((None, None, P, MLA_RANK), lambda b: (layer, b, 0, 0)),
                  pl.BlockSpec((None, None, P, MLA_ROPE), lambda b: (layer, b, 0, 0)),
                  pl.BlockSpec(w_ukv.shape, lambda b: (0, 0))],
        out_specs=pl.BlockSpec((None, T, H * MLA_V), lambda b: (b, 0, 0)),
        out_shape=jax.ShapeDtypeStruct((B, T, H * MLA_V), jnp.float32),
        scratch_shapes=[pltpu.VMEM((P, w_ukv.shape[1]), jnp.bfloat16)],
        compiler_params=_params(("arbitrary",)),
        name="mla_sample",
    )(qm, km, vm, cache_ckv, cache_kr, w_ukv)


def _neg_tri(n):
    r = lax.broadcasted_iota(jnp.int32, (n, n), 0)
    c = lax.broadcasted_iota(jnp.int32, (n, n), 1)
    return jnp.where(r >= c, -1.0, 0.0).astype(jnp.bfloat16)


def _sb_init(q, qh_ref, acc_ref, car_ref):
    for h in range(N_HEADS):
        qh_ref[h] = jnp.where(_head_mask(HEADS_W, h), q, jnp.zeros_like(q))
    acc_ref[...] = jnp.zeros(acc_ref.shape, jnp.float32)
    car_ref[...] = jnp.zeros(car_ref.shape, jnp.float32)


def _sb_tile(qh_ref, k, v, acc_ref, car_ref, ntri, mask, z_ref=None, next_k=None, zn_ref=None):
    def scores(h, keys=k):
        return lax.dot_general(qh_ref[h], keys, _DN_T, preferred_element_type=jnp.float32)

    def cumsum(z):
        neg_abs = lax.bitcast_convert_type(lax.bitcast_convert_type(z, jnp.uint32) | jnp.uint32(0x80000000),
                                           jnp.float32)
        sp = jnp.maximum(z, 0.0) + jnp.log2(1.0 + jnp.exp2(neg_abs))
        if mask is not None:
            sp = jnp.where(mask, sp, 0.0)
        return jnp.dot(sp.astype(jnp.bfloat16), ntri, preferred_element_type=jnp.float32)

    def accumulate(h, z, inner):
        a = jnp.minimum(jnp.exp2(z + inner + car_ref[h]), 1.0)
        if mask is not None:
            a = jnp.where(mask, a, 0.0)
        acc_ref[h] += jnp.dot(a.astype(jnp.bfloat16), v, preferred_element_type=jnp.float32)
        car_ref[h] += inner[:, 0:1]

    zs, inners = {}, {}
    if z_ref is not None:
        for t in range(N_HEADS + 1):
            if t < N_HEADS:
                if next_k is not None:
                    zn_ref[t] = scores(t, next_k)
                inners[t] = cumsum(z_ref[t])
            if t >= 1:
                accumulate(t - 1, z_ref[t - 1], inners.pop(t - 1))
        return
    zs[0] = scores(0)
    for t in range(1, N_HEADS + 2):
        if t < N_HEADS:
            zs[t] = scores(t)
        if 0 <= t - 1 < N_HEADS:
            inners[t - 1] = cumsum(zs[t - 1])
        if 0 <= t - 2 < N_HEADS:
            accumulate(t - 2, zs.pop(t - 2), inners.pop(t - 2))


def _sb_finish(acc_ref):
    return functools.reduce(jnp.add, [jnp.where(_head_mask(HEADS_W, h), acc_ref[h], 0.0)
                                      for h in range(N_HEADS)])


def _sb_prompt_kernel(q_ref, k_ref, v_ref, o_ref, qh_ref, acc_ref, car_ref, za_ref, zb_ref):
    i = pl.program_id(1)
    TQ, TK = TQ_SB, TK_SB
    _sb_init(q_ref[...], qh_ref, acc_ref, car_ref)
    ntri = _neg_tri(TK)
    row = i * TQ + lax.broadcasted_iota(jnp.int32, (TQ, TK), 0)
    col = lax.broadcasted_iota(jnp.int32, (TQ, TK), 1)

    def keys(t):
        return k_ref[pl.ds(pl.multiple_of(t * TK, TK), TK), :]

    def step(t, src_ref, dst_ref, masked):
        mask = (t * TK + col < row) if masked else None
        _sb_tile(qh_ref, None, v_ref[pl.ds(pl.multiple_of(t * TK, TK), TK), :], acc_ref, car_ref, ntri, mask,
                 z_ref=src_ref, next_k=keys(jnp.maximum(t - 1, 0)), zn_ref=dst_ref)

    first = 2 * i + 1
    for h in range(N_HEADS):
        za_ref[h] = lax.dot_general(qh_ref[h], keys(first), _DN_T, preferred_element_type=jnp.float32)
    step(first, za_ref, zb_ref, True)
    step(first - 1, zb_ref, za_ref, True)

    def pair(n, c):
        t = first - 2 - 2 * n
        step(t, za_ref, zb_ref, False)
        step(t - 1, zb_ref, za_ref, False)
        return c

    lax.fori_loop(0, i, pair, 0)
    o_ref[...] = _sb_finish(acc_ref)


def _sb_prompt(q, k, v):
    B, S, W = q.shape
    TQ, TK = TQ_SB, TK_SB
    assert S % TQ == 0 and TQ == 2 * TK
    return pl.pallas_call(
        _sb_prompt_kernel,
        grid=(B, S // TQ),
        in_specs=[pl.BlockSpec((None, TQ, W), lambda b, i: (b, i, 0)),
                  pl.BlockSpec((None, S, W), lambda b, i: (b, 0, 0)),
                  pl.BlockSpec((None, S, W), lambda b, i: (b, 0, 0))],
        out_specs=pl.BlockSpec((None, TQ, W), lambda b, i: (b, i, 0)),
        out_shape=jax.ShapeDtypeStruct((B, S, W), jnp.float32),
        scratch_shapes=[pltpu.VMEM((N_HEADS, TQ, W), jnp.bfloat16), pltpu.VMEM((N_HEADS, TQ, W), jnp.float32),
                        pltpu.VMEM((N_HEADS, TQ, 1), jnp.float32),
                        pltpu.VMEM((N_HEADS, TQ, TK), jnp.float32), pltpu.VMEM((N_HEADS, TQ, TK), jnp.float32)],
        compiler_params=_params(("arbitrary", "arbitrary")),
        name="sb_prompt",
    )(q, k, v)


def _sb_sample_kernel(q_ref, kn_ref, vn_ref, ck_ref, cv_ref, o_ref, kb_ref, vb_ref, qh_ref, acc_ref, car_ref):
    bf = jnp.bfloat16
    T = q_ref.shape[0]
    P = ck_ref.shape[0]
    TK = min(TK_SB, P)
    kb_ref[...] = ck_ref[...].astype(bf)
    vb_ref[...] = cv_ref[...].astype(bf)
    _sb_init(q_ref[...], qh_ref, acc_ref, car_ref)
    r = lax.broadcasted_iota(jnp.int32, (T, T), 0)
    c = lax.broadcasted_iota(jnp.int32, (T, T), 1)
    _sb_tile(qh_ref, kn_ref[...], vn_ref[...], acc_ref, car_ref, _neg_tri(T), c < r)
    ntri = _neg_tri(TK)

    def body(n, carry):
        start = pl.multiple_of(P - (n + 1) * TK, TK)
        _sb_tile(qh_ref, kb_ref[pl.ds(start, TK), :], vb_ref[pl.ds(start, TK), :], acc_ref, car_ref, ntri, None)
        return carry

    lax.fori_loop(0, P // TK, body, 0)
    o_ref[...] = _sb_finish(acc_ref)


def _sb_sample(q, k_new, v_new, cache_k, cache_v, layer):
    B, T, W = q.shape
    P = cache_k.shape[2]
    assert P % min(TK_SB, P) == 0
    new = pl.BlockSpec((None, T, W), lambda b: (b, 0, 0))
    cache = pl.BlockSpec((None, None, P, W), lambda b: (layer, b, 0, 0))
    return pl.pallas_call(
        _sb_sample_kernel,
        grid=(B,),
        in_specs=[new, new, new, cache, cache],
        out_specs=new,
        out_shape=jax.ShapeDtypeStruct((B, T, W), jnp.float32),
        scratch_shapes=[pltpu.VMEM((P, W), jnp.bfloat16), pltpu.VMEM((P, W), jnp.bfloat16),
                        pltpu.VMEM((N_HEADS, T, W), jnp.bfloat16), pltpu.VMEM((N_HEADS, T, W), jnp.float32),
                        pltpu.VMEM((N_HEADS, T, 1), jnp.float32)],
        compiler_params=_params(("arbitrary",)),
        name="sb_sample",
    )(q, k_new, v_new, cache_k, cache_v)


def _merge_ffn_kernel(final, x_ref, oa_ref, om_ref, os_ref, goa_ref, gom_ref, gos_ref, wout_ref,
                      gffn_ref, wup_ref, wdown_ref, gfin_ref, y_ref, x1_ref, h_ref, acc_ref):
    bf = jnp.bfloat16
    j = pl.program_id(1)

    @pl.when(j == 0)
    def _():
        cat = jnp.concatenate([_rms(oa_ref[...], goa_ref[...]).astype(bf),
                               _rms(om_ref[...], gom_ref[...]).astype(bf),
                               _rms(os_ref[...], gos_ref[...]).astype(bf)], axis=-1)
        x1 = x_ref[...] + jnp.dot(cat, wout_ref[...], preferred_element_type=jnp.float32)
        x1_ref[...] = x1
        h_ref[...] = _rms(x1, gffn_ref[...]).astype(bf)
        acc_ref[...] = jnp.zeros(acc_ref.shape, jnp.float32)

    u = jnp.maximum(jnp.dot(h_ref[...], wup_ref[...], preferred_element_type=jnp.float32), 0.0)
    acc_ref[...] += jnp.dot((u * u).astype(bf), wdown_ref[...], preferred_element_type=jnp.float32)

    @pl.when(j == pl.num_programs(1) - 1)
    def _():
        y = x1_ref[...] + acc_ref[...]
        y_ref[...] = _rms(y, gfin_ref[...]) if final else y


def _merge_ffn(x, oa, om, osb, g_oa, g_om, g_os, w_out, g_ffn, w_up, w_down, g_final, final):
    N, D = x.shape
    F = w_up.shape[1]
    tm = min(TM_FFN, N)
    tf = min(TF_FFN, F)
    assert N % tm == 0 and F % tf == 0
    row = lambda w: pl.BlockSpec((tm, w), lambda i, j: (i, 0))
    full = lambda a: pl.BlockSpec(a.shape, lambda i, j: (0,) * a.ndim)
    return pl.pallas_call(
        functools.partial(_merge_ffn_kernel, final),
        grid=(N // tm, F // tf),
        in_specs=[row(D), row(oa.shape[1]), row(om.shape[1]), row(osb.shape[1]),
                  full(g_oa), full(g_om), full(g_os), full(w_out), full(g_ffn),
                  pl.BlockSpec((D, tf), lambda i, j: (0, j)),
                  pl.BlockSpec((tf, D), lambda i, j: (j, 0)),
                  full(g_final)],
        out_specs=row(D),
        out_shape=jax.ShapeDtypeStruct((N, D), jnp.float32),
        scratch_shapes=[pltpu.VMEM((tm, D), jnp.float32), pltpu.VMEM((tm, D), jnp.bfloat16),
                        pltpu.VMEM((tm, D), jnp.float32)],
        compiler_params=_params(("arbitrary", "arbitrary")),
        name="merge_ffn",
    )(x, oa, om, osb, g_oa, g_om, g_os, w_out, g_ffn, w_up, w_down, g_final)


def _swap_halves(w):
    half = w.shape[-1] // 2
    return jnp.concatenate([w[..., half:], w[..., :half]], axis=-1)


def _pad_cols(w, width):
    return jnp.pad(w, [(0, 0)] * (w.ndim - 1) + [(0, width - w.shape[-1])])


def _prep_layer(w_in, w_uq, w_ukv):
    bf = jnp.bfloat16
    W = HEADS_W
    a_end = 3 * W
    cq_end = a_end + MLA_RANK
    ckv_end = cq_end + MLA_RANK
    kr_end = ckv_end + MLA_ROPE
    w_in = w_in.astype(bf)
    w_kr = w_in[:, ckv_end:kr_end]
    n_main = w_in.shape[1] - MLA_ROPE
    w_in_p = jnp.zeros((w_in.shape[0], n_main + 2 * ROPE_PAD), bf)
    w_in_p = w_in_p.at[:, :ckv_end].set(w_in[:, :ckv_end])
    w_in_p = w_in_p.at[:, ckv_end:n_main].set(w_in[:, kr_end:])
    w_in_p = w_in_p.at[:, n_main:n_main + MLA_ROPE].set(w_kr)
    w_in_p = w_in_p.at[:, n_main + ROPE_PAD:n_main + ROPE_PAD + MLA_ROPE].set(_swap_halves(w_kr))
    qn = w_uq[:, :, :MLA_NOPE].reshape(MLA_RANK, N_HEADS * MLA_NOPE)
    qp = w_uq[:, :, MLA_NOPE:]
    w_uq_p = jnp.concatenate([qn, _pad_cols(qp, ROPE_PAD).reshape(MLA_RANK, -1),
                              _pad_cols(_swap_halves(qp), ROPE_PAD).reshape(MLA_RANK, -1)], axis=1)
    w_ukv_p = jnp.concatenate([w_ukv[:, :, :MLA_NOPE].reshape(MLA_RANK, -1),
                               w_ukv[:, :, MLA_NOPE:].reshape(MLA_RANK, -1)], axis=1)
    return w_in_p.astype(bf), w_uq_p.astype(bf), w_ukv_p.astype(bf)


def _rope_tables(pos):
    half = MLA_ROPE // 2
    inv = ROPE_THETA ** (-jnp.arange(half, dtype=jnp.float32) / half)
    ang = pos.astype(jnp.float32)[:, None] * inv[None, :]
    cos, sin = jnp.cos(ang), jnp.sin(ang)
    return (_pad_cols(jnp.concatenate([cos, cos], axis=-1), ROPE_PAD),
            _pad_cols(jnp.concatenate([-sin, sin], axis=-1), ROPE_PAD))


def _band_bias_chunk(rel):
    R = rel.shape[-1]
    assert R == CHUNK + REL_MAX
    K = BAND + CHUNK
    L = K + CHUNK - 1
    lead = rel.shape[:-1]
    g = jnp.concatenate([rel, jnp.broadcast_to(rel[..., R - 1:], lead + (L - R,))], axis=-1)
    g = _pad_cols(jnp.flip(g, axis=-1), L + 1)
    m = jnp.tile(g, (1,) * len(lead) + (CHUNK,))[..., :CHUNK * L].reshape(lead + (CHUNK, L))
    return m[..., CHUNK - 1:CHUNK - 1 + K]


def _band_bias_prompt(chunk_bias):
    K = chunk_bias.shape[-1]
    out = jnp.full(chunk_bias.shape[:-2] + (TQ_BAND, BAND + TQ_BAND), NEG, jnp.float32)
    for c in range(TQ_BAND // CHUNK):
        out = out.at[..., CHUNK * c:CHUNK * (c + 1), CHUNK * c:CHUNK * c + K].set(chunk_bias)
    return out


def kernel(x_prompt, x_sample, cache_a_k, cache_a_v, cache_mla_ckv, cache_mla_krope, cache_sb_k, cache_sb_v,
           g_mix, w_in, g_cq, g_ckv, w_uq, w_ukv, a_rel_bias, g_out_a, g_out_mla, g_out_sb, w_out,
           g_ffn, w_up, w_down, g_final):
    bf = jnp.bfloat16
    B, S, D = x_prompt.shape
    DB, T, _ = x_sample.shape
    depth = w_in.shape[0]
    LA = cache_a_k.shape[2]
    P = cache_mla_ckv.shape[2]
    lc = min(BAND, S)
    assert T == CHUNK and P % CHUNK == 0 and LA == BAND and lc == BAND

    cache_a_k = cache_a_k.reshape(depth, DB, LA, HEADS_W)
    cache_a_v = cache_a_v.reshape(depth, DB, LA, HEADS_W)
    cache_sb_k = cache_sb_k.reshape(depth, DB, P, HEADS_W)
    cache_sb_v = cache_sb_v.reshape(depth, DB, P, HEADS_W)

    tm_s = min(TM_IN, DB * T)
    cos_p, sin_p = _rope_tables(jnp.arange(S))
    cos_s, sin_s = _rope_tables(P + jnp.arange(tm_s) % T)
    row = lambda g: g.reshape(1, -1)
    chunk_bias = _band_bias_chunk(a_rel_bias)
    bias_p = _band_bias_prompt(chunk_bias)

    xp = x_prompt.reshape(B * S, D)
    xs = x_sample.reshape(DB * T, D)
    p_states, s_states = [], []
    for l in range(depth):
        w_in_p, w_uq_p, w_ukv_p = _prep_layer(w_in[l], w_uq[l], w_ukv[l])
        w_out_b, w_up_b, w_down_b = w_out[l].astype(bf), w_up[l].astype(bf), w_down[l].astype(bf)
        last = l == depth - 1
        lw_in = (row(g_mix[l]), w_in_p, row(g_cq[l]), row(g_ckv[l]), w_uq_p, w_ukv_p)
        lw_out = (row(g_out_a[l]), row(g_out_mla[l]), row(g_out_sb[l]), w_out_b, row(g_ffn[l]),
                  w_up_b, w_down_b, row(g_final), last)

        (qa, ka, va, kaf, vaf, ckv, kr, qc, kc, vc, kcf, vcf, qm, km, _, vmt) = _in_proj(
            xp, *lw_in, cos_p, sin_p, S, lc)
        seq = lambda a: a.reshape(B, S, a.shape[-1])
        oa = _band_prompt(seq(qa), seq(ka), seq(va), bias_p[l])
        om = _mla_prompt(qm.reshape(N_HEADS, B, S, MLA_QK), km.reshape(N_HEADS, B, S, MLA_QK),
                         vmt.reshape(B, S // T_MLA, N_HEADS * MLA_V, T_MLA))
        osb = _sb_prompt(seq(qc), seq(kc), seq(vc))
        xp = _merge_ffn(xp, oa.reshape(B * S, -1), om.reshape(B * S, -1), osb.reshape(B * S, -1), *lw_out)
        p_states.append((kaf.reshape(B, lc, N_HEADS, HEAD_DIM), vaf.reshape(B, lc, N_HEADS, HEAD_DIM),
                         ckv.reshape(B, S, MLA_RANK), kr.reshape(B, S, MLA_ROPE),
                         kcf.reshape(B, S, N_HEADS, HEAD_DIM), vcf.reshape(B, S, N_HEADS, HEAD_DIM)))

        (qa, ka, va, kaf, vaf, ckv, kr, qc, kc, vc, kcf, vcf, qm, km, vm, _) = _in_proj(
            xs, *lw_in, cos_s, sin_s, T, T)
        seq = lambda a: a.reshape(DB, T, a.shape[-1])
        bias_c, bias_n = chunk_bias[l, :, :, :LA], chunk_bias[l, :, :, LA:]
        oa, sk, sv = _band_sample(seq(qa), seq(kaf), seq(vaf), cache_a_k, cache_a_v, l, bias_c, bias_n)
        om = _mla_sample(qm.reshape(N_HEADS, DB, T, MLA_QK), km.reshape(N_HEADS, DB, T, MLA_QK), seq(vm),
                         cache_mla_ckv, cache_mla_krope, l, w_ukv_p)
        osb = _sb_sample(seq(qc), seq(kc), seq(vc), cache_sb_k, cache_sb_v, l)
        xs = _merge_ffn(xs, oa.reshape(DB * T, -1), om.reshape(DB * T, -1), osb.reshape(DB * T, -1), *lw_out)
        s_states.append((sk.reshape(DB, LA, N_HEADS, HEAD_DIM), sv.reshape(DB, LA, N_HEADS, HEAD_DIM),
                         ckv.reshape(DB, T, MLA_RANK), kr.reshape(DB, T, MLA_ROPE),
                         kcf.reshape(DB, T, N_HEADS, HEAD_DIM), vcf.reshape(DB, T, N_HEADS, HEAD_DIM)))

    p_out = [jnp.stack(t, axis=0) for t in zip(*p_states)]
    s_out = [jnp.stack(t, axis=0) for t in zip(*s_states)]
    return (xp.reshape(B, S, D), xs.reshape(DB, T, D), *p_out, *s_out)
```

```python
import functools
import math

import numpy as np
import jax
import jax.numpy as jnp
from jax import lax
from jax.experimental import pallas as pl
from jax.experimental.pallas import tpu as pltpu

CHUNK = 64
HEAD_DIM = 64
N_HEADS = 4
BAND = 8 * CHUNK
REL_MAX = 128
MLA_NOPE = 128
MLA_ROPE = 64
MLA_V = 128
MLA_RANK = 256
ROPE_THETA = 10000.0
EPS = 1e-6
NEG = -1e30

HEADS_W = N_HEADS * HEAD_DIM
MLA_QK = 256
ROPE_PAD = 128

TQ_BAND = 256
TM_IN = 512
T_MLA = TM_IN
HP_MLA = 2
TQ_SB = 512
TK_SB = 256
LOG2E = math.log2(math.e)
TM_FFN = 512
TF_FFN = 1024
VMEM_LIMIT = 56 * 1024 * 1024

_DN_T = (((1,), (1,)), ((), ()))


def _params(sem, flags=None):
    return pltpu.CompilerParams(dimension_semantics=sem, vmem_limit_bytes=VMEM_LIMIT, flags=flags)


def _rms(x, g):
    return x * lax.rsqrt(jnp.mean(x * x, axis=-1, keepdims=True) + EPS) * g


def _head_mask(width, h):
    lane = lax.broadcasted_iota(jnp.int32, (1, width), 1)
    return (lane // HEAD_DIM) == h


def _in_proj_kernel(x_ref, g_ref, w_ref, gcq_ref, gckv_ref, wuq_ref, wukv_ref, cos_ref, sin_ref,
                    qa_ref, ka_ref, va_ref, kaf_ref, vaf_ref, ckv_ref, kr_ref,
                    qc_ref, kc_ref, vc_ref, kcf_ref, vcf_ref, qm_ref, km_ref, vm_ref, vmt_ref, qmt_ref):
    bf = jnp.bfloat16
    h = _rms(x_ref[...], g_ref[...]).astype(bf)
    p = jnp.dot(h, w_ref[...], preferred_element_type=jnp.float32)
    W = HEADS_W
    qa_ref[...] = (p[:, 0:W] * HEAD_DIM ** -0.5).astype(bf)
    ka = p[:, W:2 * W]
    va = p[:, 2 * W:3 * W]
    ka_ref[...] = ka.astype(bf)
    va_ref[...] = va.astype(bf)
    kaf_ref[...] = ka
    vaf_ref[...] = va
    cq = _rms(p[:, 3 * W:4 * W], gcq_ref[...])
    ckv = _rms(p[:, 4 * W:5 * W], gckv_ref[...])
    ckv_ref[...] = ckv
    qc_ref[...] = (p[:, 5 * W:6 * W] * (HEAD_DIM ** -0.5 * LOG2E)).astype(bf)
    kc = p[:, 6 * W:7 * W]
    vc = p[:, 7 * W:8 * W]
    kc_ref[...] = kc.astype(bf)
    vc_ref[...] = vc.astype(bf)
    kcf_ref[...] = kc
    vcf_ref[...] = vc
    cos = cos_ref[...]
    sin = sin_ref[...]
    kr = p[:, 8 * W:8 * W + ROPE_PAD] * cos + p[:, 8 * W + ROPE_PAD:8 * W + 2 * ROPE_PAD] * sin
    kr_ref[...] = kr[:, :MLA_ROPE]
    krb = kr.astype(bf)
    q = jnp.dot(cq.astype(bf), wuq_ref[...], preferred_element_type=jnp.float32)
    kv = jnp.dot(ckv.astype(bf), wukv_ref[...], preferred_element_type=jnp.float32)
    scale = (MLA_NOPE + MLA_ROPE) ** -0.5 * LOG2E
    nq = N_HEADS * MLA_NOPE
    for hh in range(N_HEADS):
        qn = q[:, hh * MLA_NOPE:(hh + 1) * MLA_NOPE]
        qp = q[:, nq + hh * ROPE_PAD:nq + (hh + 1) * ROPE_PAD]
        qs = q[:, nq + (N_HEADS + hh) * ROPE_PAD:nq + (N_HEADS + hh + 1) * ROPE_PAD]
        qh = jnp.concatenate([qn * scale, (qp * cos + qs * sin) * scale], axis=-1)
        qm_ref[hh] = qh.astype(bf)
        qmt_ref[hh] = qh.T.astype(bf)
        km_ref[hh, :, 0:MLA_NOPE] = kv[:, hh * MLA_NOPE:(hh + 1) * MLA_NOPE].astype(bf)
        km_ref[hh, :, MLA_NOPE:MLA_QK] = krb
    vm_ref[...] = kv[:, nq:].astype(bf)
    vmt_ref[...] = kv[:, nq:].T.astype(bf)


def _in_proj(x, g, w_in, g_cq, g_ckv, w_uq, w_ukv, cos, sin, rows_per_seq, tail_rows):
    N, D = x.shape
    tm = min(TM_IN, N)
    assert N % tm == 0 and cos.shape[0] % tm == 0
    n_tab = cos.shape[0] // tm
    n_seq = N // rows_per_seq
    if tail_rows == rows_per_seq:
        tail_map = lambda i: (i, 0)
    else:
        assert tail_rows % tm == 0 and rows_per_seq % tm == 0
        tiles_per_seq = rows_per_seq // tm
        tail_tiles = tail_rows // tm

        def tail_map(i):
            b = i // tiles_per_seq
            t = i % tiles_per_seq
            return (b * tail_tiles + jnp.maximum(t - (tiles_per_seq - tail_tiles), 0), 0)

    row = lambda w: pl.BlockSpec((tm, w), lambda i: (i, 0))
    full = lambda a: pl.BlockSpec(a.shape, lambda i: (0,) * a.ndim)
    tab = pl.BlockSpec((tm, ROPE_PAD), lambda i: (i % n_tab, 0))
    tail = pl.BlockSpec((tm, HEADS_W), tail_map)
    heads = pl.BlockSpec((N_HEADS, tm, MLA_QK), lambda i: (0, i, 0))
    bf, f32 = jnp.bfloat16, jnp.float32
    sds = jax.ShapeDtypeStruct
    out_shape = [
        sds((N, HEADS_W), bf), sds((N, HEADS_W), bf), sds((N, HEADS_W), bf),
        sds((n_seq * tail_rows, HEADS_W), f32), sds((n_seq * tail_rows, HEADS_W), f32),
        sds((N, MLA_RANK), f32), sds((N, MLA_ROPE), f32),
        sds((N, HEADS_W), bf), sds((N, HEADS_W), bf), sds((N, HEADS_W), bf),
        sds((N, HEADS_W), f32), sds((N, HEADS_W), f32),
        sds((N_HEADS, N, MLA_QK), bf), sds((N_HEADS, N, MLA_QK), bf),
        sds((N, N_HEADS * MLA_V), bf),
        sds((N // tm, N_HEADS * MLA_V, tm), bf),
        sds((N_HEADS, N // tm, MLA_QK, tm), bf),
    ]
    out_specs = [row(HEADS_W), row(HEADS_W), row(HEADS_W), tail, tail,
                 row(MLA_RANK), row(MLA_ROPE),
                 row(HEADS_W), row(HEADS_W), row(HEADS_W), row(HEADS_W), row(HEADS_W),
                 heads, heads, row(N_HEADS * MLA_V),
                 pl.BlockSpec((None, N_HEADS * MLA_V, tm), lambda i: (i, 0, 0)),
                 pl.BlockSpec((N_HEADS, None, MLA_QK, tm), lambda i: (0, i, 0, 0))]
    return pl.pallas_call(
        _in_proj_kernel,
        grid=(N // tm,),
        in_specs=[row(D), full(g), full(w_in), full(g_cq), full(g_ckv), full(w_uq), full(w_ukv), tab, tab],
        out_specs=out_specs,
        out_shape=out_shape,
        compiler_params=_params(("arbitrary",)),
        name="in_proj",
    )(x, g, w_in, g_cq, g_ckv, w_uq, w_ukv, cos, sin)


def _band_heads(q, kparts, vparts, biases, valid):
    out = jnp.zeros((q.shape[0], HEADS_W), jnp.float32)
    hms = [_head_mask(HEADS_W, h) for h in range(N_HEADS)]
    raw = [[lax.dot_general(jnp.where(hm, q, jnp.zeros_like(q)), k, _DN_T, preferred_element_type=jnp.float32)
            for k in kparts] for hm in hms]
    for h in range(N_HEADS):
        hm = hms[h]
        ss = []
        for i in range(len(kparts)):
            s = raw[h][i] + biases[i][h]
            if valid is not None and valid[i] is not None:
                s = jnp.where(valid[i], s, NEG)
            ss.append(s)
        m = functools.reduce(jnp.maximum, [jnp.max(s, axis=-1, keepdims=True) for s in ss])
        ps = [jnp.exp(s - m) for s in ss]
        den = functools.reduce(jnp.add, [jnp.sum(p, axis=-1, keepdims=True) for p in ps])
        o = functools.reduce(jnp.add, [jnp.dot(p.astype(jnp.bfloat16), v, preferred_element_type=jnp.float32)
                                       for p, v in zip(ps, vparts)])
        out = out + jnp.where(hm, o / den, 0.0)
    return out


def _band_prompt_kernel(q_ref, k_ref, v_ref, bias_ref, o_ref):
    i = pl.program_id(1)
    T = TQ_BAND
    nprev = BAND // T
    starts = [pl.multiple_of(jnp.maximum(i - (nprev - j), 0) * T, T) for j in range(nprev + 1)]
    k = jnp.concatenate([k_ref[pl.ds(s, T), :] for s in starts], axis=0)
    v = jnp.concatenate([v_ref[pl.ds(s, T), :] for s in starts], axis=0)
    col = lax.broadcasted_iota(jnp.int32, (1, BAND + T), 1)
    valid = col >= BAND - i * T
    o_ref[...] = _band_heads(q_ref[...], [k], [v], [bias_ref], [valid])


def _band_prompt(q, k, v, bias):
    B, S, W = q.shape
    T = TQ_BAND
    assert S % T == 0 and BAND % T == 0
    return pl.pallas_call(
        _band_prompt_kernel,
        grid=(B, S // T),
        in_specs=[pl.BlockSpec((None, T, W), lambda b, i: (b, i, 0)),
                  pl.BlockSpec((None, S, W), lambda b, i: (b, 0, 0)),
                  pl.BlockSpec((None, S, W), lambda b, i: (b, 0, 0)),
                  pl.BlockSpec(bias.shape, lambda b, i: (0, 0, 0))],
        out_specs=pl.BlockSpec((None, T, W), lambda b, i: (b, i, 0)),
        out_shape=jax.ShapeDtypeStruct((B, S, W), jnp.float32),
        compiler_params=_params(("arbitrary", "arbitrary")),
        name="band_prompt",
    )(q, k, v, bias)


def _band_sample_kernel(q_ref, kn_ref, vn_ref, ck_ref, cv_ref, bc_ref, bn_ref, o_ref, sk_ref, sv_ref):
    bf = jnp.bfloat16
    ck, cv, kn, vn = ck_ref[...], cv_ref[...], kn_ref[...], vn_ref[...]
    o_ref[...] = _band_heads(q_ref[...], [ck.astype(bf), kn.astype(bf)], [cv.astype(bf), vn.astype(bf)],
                             [bc_ref, bn_ref], None)
    n_keep = ck.shape[0] - kn.shape[0]
    sk_ref[0:n_keep, :] = ck[kn.shape[0]:, :]
    sk_ref[n_keep:, :] = kn
    sv_ref[0:n_keep, :] = cv[vn.shape[0]:, :]
    sv_ref[n_keep:, :] = vn


def _band_sample(q, k_new, v_new, cache_k, cache_v, layer, bias_c, bias_n):
    B, T, W = q.shape
    LA = cache_k.shape[2]
    new = pl.BlockSpec((None, T, W), lambda b: (b, 0, 0))
    cache = pl.BlockSpec((None, None, LA, W), lambda b: (layer, b, 0, 0))
    roll = pl.BlockSpec((None, LA, W), lambda b: (b, 0, 0))
    full = lambda a: pl.BlockSpec(a.shape, lambda b: (0,) * a.ndim)
    return pl.pallas_call(
        _band_sample_kernel,
        grid=(B,),
        in_specs=[new, new, new, cache, cache, full(bias_c), full(bias_n)],
        out_specs=[new, roll, roll],
        out_shape=[jax.ShapeDtypeStruct((B, T, W), jnp.float32),
                   jax.ShapeDtypeStruct((B, LA, W), jnp.float32),
                   jax.ShapeDtypeStruct((B, LA, W), jnp.float32)],
        compiler_params=_params(("arbitrary",)),
        name="band_sample",
    )(q, k_new, v_new, cache_k, cache_v, bias_c, bias_n)


def _mla_prompt_kernel(qt_ref, k_ref, vt_ref, o_ref, m_ref, l_ref, acc_ref, sa_ref, sb_ref):
    i = pl.program_id(2)
    TQ = TK = T_MLA
    m_ref[...] = jnp.full(m_ref.shape, NEG, jnp.float32)
    l_ref[...] = jnp.zeros(l_ref.shape, jnp.float32)
    acc_ref[...] = jnp.zeros(acc_ref.shape, jnp.float32)

    def scores(hh, j):
        start = pl.multiple_of(j * TK, TK)
        return jnp.dot(k_ref[hh, pl.ds(start, TK), :], qt_ref[hh], preferred_element_type=jnp.float32)

    def step(j, src_ref, dst_ref, mask):
        for hh in range(HP_MLA):
            st = src_ref[hh]
            if mask is not None:
                st = jnp.where(mask, st, NEG)
            m_old = m_ref[hh]
            m_new = jnp.maximum(m_old, jnp.max(st, axis=0, keepdims=True))
            alpha = jnp.exp2(m_old - m_new)
            pt = jnp.exp2(st - m_new)
            l_ref[hh] = alpha * l_ref[hh] + jnp.sum(pt, axis=0, keepdims=True)
            if dst_ref is not None:
                dst_ref[hh] = scores(hh, j + 1)
            vt = vt_ref[j, hh * MLA_V:(hh + 1) * MLA_V, :]
            acc_ref[hh] = alpha * acc_ref[hh] + jnp.dot(vt, pt.astype(jnp.bfloat16),
                                                        preferred_element_type=jnp.float32)
            m_ref[hh] = m_new

    def pair(n, c):
        step(2 * n, sa_ref, sb_ref, None)
        step(2 * n + 1, sb_ref, sa_ref, None)
        return c

    r = lax.broadcasted_iota(jnp.int32, (TK, TQ), 0) // CHUNK
    c = lax.broadcasted_iota(jnp.int32, (TK, TQ), 1) // CHUNK
    diag = r <= c
    for hh in range(HP_MLA):
        sa_ref[hh] = scores(hh, 0)
    lax.fori_loop(0, i // 2, pair, 0)

    @pl.when(i % 2 == 0)
    def _():
        step(i, sa_ref, None, diag)

    @pl.when(i % 2 == 1)
    def _():
        step(i - 1, sa_ref, sb_ref, None)
        step(i, sb_ref, None, diag)

    for hh in range(HP_MLA):
        o_ref[:, hh * MLA_V:(hh + 1) * MLA_V] = (acc_ref[hh] / l_ref[hh]).T


def _mla_prompt(qmt, km, vmt):
    H, B, S, E = km.shape
    TQ = TK = T_MLA
    HP = HP_MLA
    assert S % TQ == 0 and H % HP == 0 and vmt.shape == (B, S // TK, H * MLA_V, TK)
    assert qmt.shape == (H, B, S // TQ, E, TQ)
    return pl.pallas_call(
        _mla_prompt_kernel,
        grid=(B, H // HP, S // TQ),
        in_specs=[pl.BlockSpec((HP, None, None, E, TQ), lambda b, h, i: (h, b, i, 0, 0)),
                  pl.BlockSpec((HP, None, S, E), lambda b, h, i: (h, b, 0, 0)),
                  pl.BlockSpec((None, S // TK, HP * MLA_V, TK), lambda b, h, i: (b, 0, h, 0))],
        out_specs=pl.BlockSpec((None, TQ, HP * MLA_V), lambda b, h, i: (b, i, h)),
        out_shape=jax.ShapeDtypeStruct((B, S, H * MLA_V), jnp.float32),
        scratch_shapes=[pltpu.VMEM((HP, 1, TQ), jnp.float32), pltpu.VMEM((HP, 1, TQ), jnp.float32),
                        pltpu.VMEM((HP, MLA_V, TQ), jnp.float32),
                        pltpu.VMEM((HP, TK, TQ), jnp.float32), pltpu.VMEM((HP, TK, TQ), jnp.float32)],
        compiler_params=_params(("arbitrary", "arbitrary", "arbitrary")),
        name="mla_prompt",
    )(qmt, km, vmt)


def _mla_sample_kernel(q_ref, kn_ref, vn_ref, ckv_ref, ckr_ref, wukv_ref, o_ref, kv_ref):
    bf = jnp.bfloat16
    P = ckv_ref.shape[0]
    step = min(512, P)
    for r in range(0, P, step):
        kv_ref[r:r + step, :] = jnp.dot(ckv_ref[r:r + step, :].astype(bf), wukv_ref[...],
                                        preferred_element_type=jnp.float32).astype(bf)
    ckr = ckr_ref[...].astype(bf)
    nk = N_HEADS * MLA_NOPE
    for h in range(N_HEADS):
        q = q_ref[h]
        s_c = (lax.dot_general(q[:, :MLA_NOPE], kv_ref[:, h * MLA_NOPE:(h + 1) * MLA_NOPE], _DN_T,
                               preferred_element_type=jnp.float32)
               + lax.dot_general(q[:, MLA_NOPE:MLA_NOPE + MLA_ROPE], ckr, _DN_T,
                                 preferred_element_type=jnp.float32))
        s_n = lax.dot_general(q, kn_ref[h], _DN_T, preferred_element_type=jnp.float32)
        m = jnp.maximum(jnp.max(s_c, axis=-1, keepdims=True), jnp.max(s_n, axis=-1, keepdims=True))
        p_c = jnp.exp2(s_c - m)
        p_n = jnp.exp2(s_n - m)
        den = jnp.sum(p_c, axis=-1, keepdims=True) + jnp.sum(p_n, axis=-1, keepdims=True)
        o = (jnp.dot(p_c.astype(bf), kv_ref[:, nk + h * MLA_V:nk + (h + 1) * MLA_V],
                     preferred_element_type=jnp.float32)
             + jnp.dot(p_n.astype(bf), vn_ref[:, h * MLA_V:(h + 1) * MLA_V],
                       preferred_element_type=jnp.float32))
        o_ref[:, h * MLA_V:(h + 1) * MLA_V] = o / den


def _mla_sample(qm, km, vm, cache_ckv, cache_kr, layer, w_ukv):
    H, B, T, E = qm.shape
    P = cache_ckv.shape[2]
    heads = pl.BlockSpec((H, None, T, E), lambda b: (0, b, 0, 0))
    return pl.pallas_call(
        _mla_sample_kernel,
        grid=(B,),
        in_specs=[heads, heads,
                  pl.BlockSpec((None, T, H * MLA_V), lambda b: (b, 0, 0)),
                  pl.BlockSpec((None, None, P, MLA_RANK), lambda b: (layer, b, 0, 0)),
                  pl.BlockSpec((None, None, P, MLA_ROPE), lambda b: (layer, b, 0, 0)),
                  pl.BlockSpec(w_ukv.shape, lambda b: (0, 0))],
        out_specs=pl.BlockSpec((None, T, H * MLA_V), lambda b: (b, 0, 0)),
        out_shape=jax.ShapeDtypeStruct((B, T, H * MLA_V), jnp.float32),
        scratch_shapes=[pltpu.VMEM((P, w_ukv.shape[1]), jnp.bfloat16)],
        compiler_params=_params(("arbitrary",)),
        name="mla_sample",
    )(qm, km, vm, cache_ckv, cache_kr, w_ukv)


def _neg_tri(n):
    r = lax.broadcasted_iota(jnp.int32, (n, n), 0)
    c = lax.broadcasted_iota(jnp.int32, (n, n), 1)
    return jnp.where(r >= c, -1.0, 0.0).astype(jnp.bfloat16)


def _sb_init(q, qh_ref, acc_ref, car_ref):
    for h in range(N_HEADS):
        qh_ref[h] = jnp.where(_head_mask(HEADS_W, h), q, jnp.zeros_like(q))
    acc_ref[...] = jnp.zeros(acc_ref.shape, jnp.float32)
    car_ref[...] = jnp.zeros(car_ref.shape, jnp.float32)


def _sb_tile(qh_ref, k, v, acc_ref, car_ref, ntri, mask, z_ref=None, next_k=None, zn_ref=None):
    def scores(h, keys=k):
        return lax.dot_general(qh_ref[h], keys, _DN_T, preferred_element_type=jnp.float32)

    def cumsum(z):
        neg_abs = lax.bitcast_convert_type(lax.bitcast_convert_type(z, jnp.uint32) | jnp.uint32(0x80000000),
                                           jnp.float32)
        sp = jnp.maximum(z, 0.0) + jnp.log2(1.0 + jnp.exp2(neg_abs))
        if mask is not None:
            sp = jnp.where(mask, sp, 0.0)
        return jnp.dot(sp.astype(jnp.bfloat16), ntri, preferred_element_type=jnp.float32)

    def accumulate(h, z, inner):
        a = jnp.minimum(jnp.exp2(z + inner + car_ref[h]), 1.0)
        if mask is not None:
            a = jnp.where(mask, a, 0.0)
        acc_ref[h] += jnp.dot(a.astype(jnp.bfloat16), v, preferred_element_type=jnp.float32)
        car_ref[h] += inner[:, 0:1]

    zs, inners = {}, {}
    if z_ref is not None:
        for t in range(N_HEADS + 1):
            if t < N_HEADS:
                if next_k is not None:
                    zn_ref[t] = scores(t, next_k)
                inners[t] = cumsum(z_ref[t])
            if t >= 1:
                accumulate(t - 1, z_ref[t - 1], inners.pop(t - 1))
        return
    zs[0] = scores(0)
    for t in range(1, N_HEADS + 2):
        if t < N_HEADS:
            zs[t] = scores(t)
        if 0 <= t - 1 < N_HEADS:
            inners[t - 1] = cumsum(zs[t - 1])
        if 0 <= t - 2 < N_HEADS:
            accumulate(t - 2, zs.pop(t - 2), inners.pop(t - 2))


def _sb_finish(acc_ref):
    return functools.reduce(jnp.add, [jnp.where(_head_mask(HEADS_W, h), acc_ref[h], 0.0)
                                      for h in range(N_HEADS)])


def _sb_prompt_kernel(q_ref, k_ref, v_ref, o_ref, qh_ref, acc_ref, car_ref, za_ref, zb_ref):
    i = pl.program_id(1)
    TQ, TK = TQ_SB, TK_SB
    _sb_init(q_ref[...], qh_ref, acc_ref, car_ref)
    ntri = _neg_tri(TK)
    row = i * TQ + lax.broadcasted_iota(jnp.int32, (TQ, TK), 0)
    col = lax.broadcasted_iota(jnp.int32, (TQ, TK), 1)

    def keys(t):
        return k_ref[pl.ds(pl.multiple_of(t * TK, TK), TK), :]

    def step(t, src_ref, dst_ref, masked):
        mask = (t * TK + col < row) if masked else None
        _sb_tile(qh_ref, None, v_ref[pl.ds(pl.multiple_of(t * TK, TK), TK), :], acc_ref, car_ref, ntri, mask,
                 z_ref=src_ref, next_k=keys(jnp.maximum(t - 1, 0)), zn_ref=dst_ref)

    first = 2 * i + 1
    for h in range(N_HEADS):
        za_ref[h] = lax.dot_general(qh_ref[h], keys(first), _DN_T, preferred_element_type=jnp.float32)
    step(first, za_ref, zb_ref, True)
    step(first - 1, zb_ref, za_ref, True)

    def pair(n, c):
        t = first - 2 - 2 * n
        step(t, za_ref, zb_ref, False)
        step(t - 1, zb_ref, za_ref, False)
        return c

    lax.fori_loop(0, i, pair, 0)
    o_ref[...] = _sb_finish(acc_ref)


def _sb_prompt(q, k, v):
    B, S, W = q.shape
    TQ, TK = TQ_SB, TK_SB
    assert S % TQ == 0 and TQ == 2 * TK
    return pl.pallas_call(
        _sb_prompt_kernel,
        grid=(B, S // TQ),
        in_specs=[pl.BlockSpec((None, TQ, W), lambda b, i: (b, i, 0)),
                  pl.BlockSpec((None, S, W), lambda b, i: (b, 0, 0)),
                  pl.BlockSpec((None, S, W), lambda b, i: (b, 0, 0))],
        out_specs=pl.BlockSpec((None, TQ, W), lambda b, i: (b, i, 0)),
        out_shape=jax.ShapeDtypeStruct((B, S, W), jnp.float32),
        scratch_shapes=[pltpu.VMEM((N_HEADS, TQ, W), jnp.bfloat16), pltpu.VMEM((N_HEADS, TQ, W), jnp.float32),
                        pltpu.VMEM((N_HEADS, TQ, 1), jnp.float32),
                        pltpu.VMEM((N_HEADS, TQ, TK), jnp.float32), pltpu.VMEM((N_HEADS, TQ, TK), jnp.float32)],
        compiler_params=_params(("arbitrary", "arbitrary")),
        name="sb_prompt",
    )(q, k, v)


def _sb_sample_kernel(q_ref, kn_ref, vn_ref, ck_ref, cv_ref, o_ref, kb_ref, vb_ref, qh_ref, acc_ref, car_ref):
    bf = jnp.bfloat16
    T = q_ref.shape[0]
    P = ck_ref.shape[0]
    TK = min(TK_SB, P)
    kb_ref[...] = ck_ref[...].astype(bf)
    vb_ref[...] = cv_ref[...].astype(bf)
    _sb_init(q_ref[...], qh_ref, acc_ref, car_ref)
    r = lax.broadcasted_iota(jnp.int32, (T, T), 0)
    c = lax.broadcasted_iota(jnp.int32, (T, T), 1)
    _sb_tile(qh_ref, kn_ref[...], vn_ref[...], acc_ref, car_ref, _neg_tri(T), c < r)
    ntri = _neg_tri(TK)

    def body(n, carry):
        start = pl.multiple_of(P - (n + 1) * TK, TK)
        _sb_tile(qh_ref, kb_ref[pl.ds(start, TK), :], vb_ref[pl.ds(start, TK), :], acc_ref, car_ref, ntri, None)
        return carry

    lax.fori_loop(0, P // TK, body, 0)
    o_ref[...] = _sb_finish(acc_ref)


def _sb_sample(q, k_new, v_new, cache_k, cache_v, layer):
    B, T, W = q.shape
    P = cache_k.shape[2]
    assert P % min(TK_SB, P) == 0
    new = pl.BlockSpec((None, T, W), lambda b: (b, 0, 0))
    cache = pl.BlockSpec((None, None, P, W), lambda b: (layer, b, 0, 0))
    return pl.pallas_call(
        _sb_sample_kernel,
        grid=(B,),
        in_specs=[new, new, new, cache, cache],
        out_specs=new,
        out_shape=jax.ShapeDtypeStruct((B, T, W), jnp.float32),
        scratch_shapes=[pltpu.VMEM((P, W), jnp.bfloat16), pltpu.VMEM((P, W), jnp.bfloat16),
                        pltpu.VMEM((N_HEADS, T, W), jnp.bfloat16), pltpu.VMEM((N_HEADS, T, W), jnp.float32),
                        pltpu.VMEM((N_HEADS, T, 1), jnp.float32)],
        compiler_params=_params(("arbitrary",)),
        name="sb_sample",
    )(q, k_new, v_new, cache_k, cache_v)


def _merge_ffn_kernel(final, x_ref, oa_ref, om_ref, os_ref, goa_ref, gom_ref, gos_ref, wout_ref,
                      gffn_ref, wup_ref, wdown_ref, gfin_ref, y_ref, x1_ref, h_ref, acc_ref):
    bf = jnp.bfloat16
    j = pl.program_id(1)

    @pl.when(j == 0)
    def _():
        cat = jnp.concatenate([_rms(oa_ref[...], goa_ref[...]).astype(bf),
                               _rms(om_ref[...], gom_ref[...]).astype(bf),
                               _rms(os_ref[...], gos_ref[...]).astype(bf)], axis=-1)
        x1 = x_ref[...] + jnp.dot(cat, wout_ref[...], preferred_element_type=jnp.float32)
        x1_ref[...] = x1
        h_ref[...] = _rms(x1, gffn_ref[...]).astype(bf)
        acc_ref[...] = jnp.zeros(acc_ref.shape, jnp.float32)

    u = jnp.maximum(jnp.dot(h_ref[...], wup_ref[...], preferred_element_type=jnp.float32), 0.0)
    acc_ref[...] += jnp.dot((u * u).astype(bf), wdown_ref[...], preferred_element_type=jnp.float32)

    @pl.when(j == pl.num_programs(1) - 1)
    def _():
        y = x1_ref[...] + acc_ref[...]
        y_ref[...] = _rms(y, gfin_ref[...]) if final else y


def _merge_ffn(x, oa, om, osb, g_oa, g_om, g_os, w_out, g_ffn, w_up, w_down, g_final, final):
    N, D = x.shape
    F = w_up.shape[1]
    tm = min(TM_FFN, N)
    tf = min(TF_FFN, F)
    assert N % tm == 0 and F % tf == 0
    row = lambda w: pl.BlockSpec((tm, w), lambda i, j: (i, 0))
    full = lambda a: pl.BlockSpec(a.shape, lambda i, j: (0,) * a.ndim)
    return pl.pallas_call(
        functools.partial(_merge_ffn_kernel, final),
        grid=(N // tm, F // tf),
        in_specs=[row(D), row(oa.shape[1]), row(om.shape[1]), row(osb.shape[1]),
                  full(g_oa), full(g_om), full(g_os), full(w_out), full(g_ffn),
                  pl.BlockSpec((D, tf), lambda i, j: (0, j)),
                  pl.BlockSpec((tf, D), lambda i, j: (j, 0)),
                  full(g_final)],
        out_specs=row(D),
        out_shape=jax.ShapeDtypeStruct((N, D), jnp.float32),
        scratch_shapes=[pltpu.VMEM((tm, D), jnp.float32), pltpu.VMEM((tm, D), jnp.bfloat16),
                        pltpu.VMEM((tm, D), jnp.float32)],
        compiler_params=_params(("arbitrary", "arbitrary")),
        name="merge_ffn",
    )(x, oa, om, osb, g_oa, g_om, g_os, w_out, g_ffn, w_up, w_down, g_final)


def _swap_halves(w):
    half = w.shape[-1] // 2
    return jnp.concatenate([w[..., half:], w[..., :half]], axis=-1)


def _pad_cols(w, width):
    return jnp.pad(w, [(0, 0)] * (w.ndim - 1) + [(0, width - w.shape[-1])])


def _prep_layer(w_in, w_uq, w_ukv):
    bf = jnp.bfloat16
    W = HEADS_W
    a_end = 3 * W
    cq_end = a_end + MLA_RANK
    ckv_end = cq_end + MLA_RANK
    kr_end = ckv_end + MLA_ROPE
    w_in = w_in.astype(bf)
    w_kr = w_in[:, ckv_end:kr_end]
    n_main = w_in.shape[1] - MLA_ROPE
    w_in_p = jnp.zeros((w_in.shape[0], n_main + 2 * ROPE_PAD), bf)
    w_in_p = w_in_p.at[:, :ckv_end].set(w_in[:, :ckv_end])
    w_in_p = w_in_p.at[:, ckv_end:n_main].set(w_in[:, kr_end:])
    w_in_p = w_in_p.at[:, n_main:n_main + MLA_ROPE].set(w_kr)
    w_in_p = w_in_p.at[:, n_main + ROPE_PAD:n_main + ROPE_PAD + MLA_ROPE].set(_swap_halves(w_kr))
    qn = w_uq[:, :, :MLA_NOPE].reshape(MLA_RANK, N_HEADS * MLA_NOPE)
    qp = w_uq[:, :, MLA_NOPE:]
    w_uq_p = jnp.concatenate([qn, _pad_cols(qp, ROPE_PAD).reshape(MLA_RANK, -1),
                              _pad_cols(_swap_halves(qp), ROPE_PAD).reshape(MLA_RANK, -1)], axis=1)
    w_ukv_p = jnp.concatenate([w_ukv[:, :, :MLA_NOPE].reshape(MLA_RANK, -1),
                               w_ukv[:, :, MLA_NOPE:].reshape(MLA_RANK, -1)], axis=1)
    return w_in_p.astype(bf), w_uq_p.astype(bf), w_ukv_p.astype(bf)


def _rope_tables(pos):
    half = MLA_ROPE // 2
    inv = ROPE_THETA ** (-jnp.arange(half, dtype=jnp.float32) / half)
    ang = pos.astype(jnp.float32)[:, None] * inv[None, :]
    cos, sin = jnp.cos(ang), jnp.sin(ang)
    return (_pad_cols(jnp.concatenate([cos, cos], axis=-1), ROPE_PAD),
            _pad_cols(jnp.concatenate([-sin, sin], axis=-1), ROPE_PAD))


def _band_bias_chunk(rel):
    R = rel.shape[-1]
    assert R == CHUNK + REL_MAX
    K = BAND + CHUNK
    L = K + CHUNK - 1
    lead = rel.shape[:-1]
    g = jnp.concatenate([rel, jnp.broadcast_to(rel[..., R - 1:], lead + (L - R,))], axis=-1)
    g = _pad_cols(jnp.flip(g, axis=-1), L + 1)
    m = jnp.tile(g, (1,) * len(lead) + (CHUNK,))[..., :CHUNK * L].reshape(lead + (CHUNK, L))
    return m[..., CHUNK - 1:CHUNK - 1 + K]


def _band_bias_prompt(chunk_bias):
    K = chunk_bias.shape[-1]
    out = jnp.full(chunk_bias.shape[:-2] + (TQ_BAND, BAND + TQ_BAND), NEG, jnp.float32)
    for c in range(TQ_BAND // CHUNK):
        out = out.at[..., CHUNK * c:CHUNK * (c + 1), CHUNK * c:CHUNK * c + K].set(chunk_bias)
    return out


def kernel(x_prompt, x_sample, cache_a_k, cache_a_v, cache_mla_ckv, cache_mla_krope, cache_sb_k, cache_sb_v,
           g_mix, w_in, g_cq, g_ckv, w_uq, w_ukv, a_rel_bias, g_out_a, g_out_mla, g_out_sb, w_out,
           g_ffn, w_up, w_down, g_final):
    bf = jnp.bfloat16
    B, S, D = x_prompt.shape
    DB, T, _ = x_sample.shape
    depth = w_in.shape[0]
    LA = cache_a_k.shape[2]
    P = cache_mla_ckv.shape[2]
    lc = min(BAND, S)
    assert T == CHUNK and P % CHUNK == 0 and LA == BAND and lc == BAND

    cache_a_k = cache_a_k.reshape(depth, DB, LA, HEADS_W)
    cache_a_v = cache_a_v.reshape(depth, DB, LA, HEADS_W)
    cache_sb_k = cache_sb_k.reshape(depth, DB, P, HEADS_W)
    cache_sb_v = cache_sb_v.reshape(depth, DB, P, HEADS_W)

    tm_s = min(TM_IN, DB * T)
    cos_p, sin_p = _rope_tables(jnp.arange(S))
    cos_s, sin_s = _rope_tables(P + jnp.arange(tm_s) % T)
    row = lambda g: g.reshape(1, -1)
    chunk_bias = _band_bias_chunk(a_rel_bias)
    bias_p = _band_bias_prompt(chunk_bias)

    xp = x_prompt.reshape(B * S, D)
    xs = x_sample.reshape(DB * T, D)
    p_states, s_states = [], []
    for l in range(depth):
        w_in_p, w_uq_p, w_ukv_p = _prep_layer(w_in[l], w_uq[l], w_ukv[l])
        w_out_b, w_up_b, w_down_b = w_out[l].astype(bf), w_up[l].astype(bf), w_down[l].astype(bf)
        last = l == depth - 1
        lw_in = (row(g_mix[l]), w_in_p, row(g_cq[l]), row(g_ckv[l]), w_uq_p, w_ukv_p)
        lw_out = (row(g_out_a[l]), row(g_out_mla[l]), row(g_out_sb[l]), w_out_b, row(g_ffn[l]),
                  w_up_b, w_down_b, row(g_final), last)

        (qa, ka, va, kaf, vaf, ckv, kr, qc, kc, vc, kcf, vcf, _, km, _, vmt, qmt) = _in_proj(
            xp, *lw_in, cos_p, sin_p, S, lc)
        seq = lambda a: a.reshape(B, S, a.shape[-1])
        oa = _band_prompt(seq(qa), seq(ka), seq(va), bias_p[l])
        om = _mla_prompt(qmt.reshape(N_HEADS, B, S // T_MLA, MLA_QK, T_MLA), km.reshape(N_HEADS, B, S, MLA_QK),
                         vmt.reshape(B, S // T_MLA, N_HEADS * MLA_V, T_MLA))
        osb = _sb_prompt(seq(qc), seq(kc), seq(vc))
        xp = _merge_ffn(xp, oa.reshape(B * S, -1), om.reshape(B * S, -1), osb.reshape(B * S, -1), *lw_out)
        p_states.append((kaf.reshape(B, lc, N_HEADS, HEAD_DIM), vaf.reshape(B, lc, N_HEADS, HEAD_DIM),
                         ckv.reshape(B, S, MLA_RANK), kr.reshape(B, S, MLA_ROPE),
                         kcf.reshape(B, S, N_HEADS, HEAD_DIM), vcf.reshape(B, S, N_HEADS, HEAD_DIM)))

        (qa, ka, va, kaf, vaf, ckv, kr, qc, kc, vc, kcf, vcf, qm, km, vm, _, _) = _in_proj(
            xs, *lw_in, cos_s, sin_s, T, T)
        seq = lambda a: a.reshape(DB, T, a.shape[-1])
        bias_c, bias_n = chunk_bias[l, :, :, :LA], chunk_bias[l, :, :, LA:]
        oa, sk, sv = _band_sample(seq(qa), seq(kaf), seq(vaf), cache_a_k, cache_a_v, l, bias_c, bias_n)
        om = _mla_sample(qm.reshape(N_HEADS, DB, T, MLA_QK), km.reshape(N_HEADS, DB, T, MLA_QK), seq(vm),
                         cache_mla_ckv, cache_mla_krope, l, w_ukv_p)
        osb = _sb_sample(seq(qc), seq(kc), seq(vc), cache_sb_k, cache_sb_v, l)
        xs = _merge_ffn(xs, oa.reshape(DB * T, -1), om.reshape(DB * T, -1), osb.reshape(DB * T, -1), *lw_out)
        s_states.append((sk.reshape(DB, LA, N_HEADS, HEAD_DIM), sv.reshape(DB, LA, N_HEADS, HEAD_DIM),
                         ckv.reshape(DB, T, MLA_RANK), kr.reshape(DB, T, MLA_ROPE),
                         kcf.reshape(DB, T, N_HEADS, HEAD_DIM), vcf.reshape(DB, T, N_HEADS, HEAD_DIM)))

    p_out = [jnp.stack(t, axis=0) for t in zip(*p_states)]
    s_out = [jnp.stack(t, axis=0) for t in zip(*s_states)]
    return (xp.reshape(B, S, D), xs.reshape(DB, T, D), *p_out, *s_out)
```

```python
import functools
import math

import numpy as np
import jax
import jax.numpy as jnp
from jax import lax
from jax.experimental import pallas as pl
from jax.experimental.pallas import tpu as pltpu

CHUNK = 64
HEAD_DIM = 64
N_HEADS = 4
BAND = 8 * CHUNK
REL_MAX = 128
MLA_NOPE = 128
MLA_ROPE = 64
MLA_V = 128
MLA_RANK = 256
ROPE_THETA = 10000.0
EPS = 1e-6
NEG = -1e30

HEADS_W = N_HEADS * HEAD_DIM
MLA_QK = 256
ROPE_PAD = 128

TQ_BAND = 256
TM_IN = 512
T_MLA = TM_IN
HP_MLA = 2
TQ_SB = 512
TK_SB = 256
LOG2E = math.log2(math.e)
TM_FFN = 1024
TF_FFN = 1024
VMEM_LIMIT = 56 * 1024 * 1024

_DN_T = (((1,), (1,)), ((), ()))


def _params(sem, flags=None):
    return pltpu.CompilerParams(dimension_semantics=sem, vmem_limit_bytes=VMEM_LIMIT, flags=flags)


def _rms(x, g):
    return x * lax.rsqrt(jnp.mean(x * x, axis=-1, keepdims=True) + EPS) * g


def _head_mask(width, h):
    lane = lax.broadcasted_iota(jnp.int32, (1, width), 1)
    return (lane // HEAD_DIM) == h


def _in_proj_kernel(x_ref, g_ref, w_ref, gcq_ref, gckv_ref, wuq_ref, wukv_ref, cos_ref, sin_ref,
                    qa_ref, ka_ref, va_ref, kaf_ref, vaf_ref, ckv_ref, kr_ref,
                    qc_ref, kc_ref, vc_ref, kcf_ref, vcf_ref, qm_ref, km_ref, vm_ref, vmt_ref, qmt_ref):
    bf = jnp.bfloat16
    h = _rms(x_ref[...], g_ref[...]).astype(bf)
    p = jnp.dot(h, w_ref[...], preferred_element_type=jnp.float32)
    W = HEADS_W
    qa_ref[...] = (p[:, 0:W] * HEAD_DIM ** -0.5).astype(bf)
    ka = p[:, W:2 * W]
    va = p[:, 2 * W:3 * W]
    ka_ref[...] = ka.astype(bf)
    va_ref[...] = va.astype(bf)
    kaf_ref[...] = ka
    vaf_ref[...] = va
    cq = _rms(p[:, 3 * W:4 * W], gcq_ref[...])
    ckv = _rms(p[:, 4 * W:5 * W], gckv_ref[...])
    ckv_ref[...] = ckv
    qc_ref[...] = (p[:, 5 * W:6 * W] * (HEAD_DIM ** -0.5 * LOG2E)).astype(bf)
    kc = p[:, 6 * W:7 * W]
    vc = p[:, 7 * W:8 * W]
    kc_ref[...] = kc.astype(bf)
    vc_ref[...] = vc.astype(bf)
    kcf_ref[...] = kc
    vcf_ref[...] = vc
    cos = cos_ref[...]
    sin = sin_ref[...]
    kr = p[:, 8 * W:8 * W + ROPE_PAD] * cos + p[:, 8 * W + ROPE_PAD:8 * W + 2 * ROPE_PAD] * sin
    kr_ref[...] = kr[:, :MLA_ROPE]
    krb = kr.astype(bf)
    q = jnp.dot(cq.astype(bf), wuq_ref[...], preferred_element_type=jnp.float32)
    kv = jnp.dot(ckv.astype(bf), wukv_ref[...], preferred_element_type=jnp.float32)
    scale = (MLA_NOPE + MLA_ROPE) ** -0.5 * LOG2E
    nq = N_HEADS * MLA_NOPE
    for hh in range(N_HEADS):
        qn = q[:, hh * MLA_NOPE:(hh + 1) * MLA_NOPE]
        qp = q[:, nq + hh * ROPE_PAD:nq + (hh + 1) * ROPE_PAD]
        qs = q[:, nq + (N_HEADS + hh) * ROPE_PAD:nq + (N_HEADS + hh + 1) * ROPE_PAD]
        qh = jnp.concatenate([qn * scale, (qp * cos + qs * sin) * scale], axis=-1)
        qm_ref[hh] = qh.astype(bf)
        qmt_ref[hh] = qh.T.astype(bf)
        km_ref[hh, :, 0:MLA_NOPE] = kv[:, hh * MLA_NOPE:(hh + 1) * MLA_NOPE].astype(bf)
        km_ref[hh, :, MLA_NOPE:MLA_QK] = krb
    vm_ref[...] = kv[:, nq:].astype(bf)
    vmt_ref[...] = kv[:, nq:].T.astype(bf)


def _in_proj(x, g, w_in, g_cq, g_ckv, w_uq, w_ukv, cos, sin, rows_per_seq, tail_rows):
    N, D = x.shape
    tm = min(TM_IN, N)
    assert N % tm == 0 and cos.shape[0] % tm == 0
    n_tab = cos.shape[0] // tm
    n_seq = N // rows_per_seq
    if tail_rows == rows_per_seq:
        tail_map = lambda i: (i, 0)
    else:
        assert tail_rows % tm == 0 and rows_per_seq % tm == 0
        tiles_per_seq = rows_per_seq // tm
        tail_tiles = tail_rows // tm

        def tail_map(i):
            b = i // tiles_per_seq
            t = i % tiles_per_seq
            return (b * tail_tiles + jnp.maximum(t - (tiles_per_seq - tail_tiles), 0), 0)

    row = lambda w: pl.BlockSpec((tm, w), lambda i: (i, 0))
    full = lambda a: pl.BlockSpec(a.shape, lambda i: (0,) * a.ndim)
    tab = pl.BlockSpec((tm, ROPE_PAD), lambda i: (i % n_tab, 0))
    tail = pl.BlockSpec((tm, HEADS_W), tail_map)
    heads = pl.BlockSpec((N_HEADS, tm, MLA_QK), lambda i: (0, i, 0))
    bf, f32 = jnp.bfloat16, jnp.float32
    sds = jax.ShapeDtypeStruct
    out_shape = [
        sds((N, HEADS_W), bf), sds((N, HEADS_W), bf), sds((N, HEADS_W), bf),
        sds((n_seq * tail_rows, HEADS_W), f32), sds((n_seq * tail_rows, HEADS_W), f32),
        sds((N, MLA_RANK), f32), sds((N, MLA_ROPE), f32),
        sds((N, HEADS_W), bf), sds((N, HEADS_W), bf), sds((N, HEADS_W), bf),
        sds((N, HEADS_W), f32), sds((N, HEADS_W), f32),
        sds((N_HEADS, N, MLA_QK), bf), sds((N_HEADS, N, MLA_QK), bf),
        sds((N, N_HEADS * MLA_V), bf),
        sds((N // tm, N_HEADS * MLA_V, tm), bf),
        sds((N_HEADS, N // tm, MLA_QK, tm), bf),
    ]
    out_specs = [row(HEADS_W), row(HEADS_W), row(HEADS_W), tail, tail,
                 row(MLA_RANK), row(MLA_ROPE),
                 row(HEADS_W), row(HEADS_W), row(HEADS_W), row(HEADS_W), row(HEADS_W),
                 heads, heads, row(N_HEADS * MLA_V),
                 pl.BlockSpec((None, N_HEADS * MLA_V, tm), lambda i: (i, 0, 0)),
                 pl.BlockSpec((N_HEADS, None, MLA_QK, tm), lambda i: (0, i, 0, 0))]
    return pl.pallas_call(
        _in_proj_kernel,
        grid=(N // tm,),
        in_specs=[row(D), full(g), full(w_in), full(g_cq), full(g_ckv), full(w_uq), full(w_ukv), tab, tab],
        out_specs=out_specs,
        out_shape=out_shape,
        compiler_params=_params(("arbitrary",)),
        name="in_proj",
    )(x, g, w_in, g_cq, g_ckv, w_uq, w_ukv, cos, sin)


def _band_heads(q, kparts, vparts, biases, valid):
    out = jnp.zeros((q.shape[0], HEADS_W), jnp.float32)
    hms = [_head_mask(HEADS_W, h) for h in range(N_HEADS)]
    raw = [[lax.dot_general(jnp.where(hm, q, jnp.zeros_like(q)), k, _DN_T, preferred_element_type=jnp.float32)
            for k in kparts] for hm in hms]
    for h in range(N_HEADS):
        hm = hms[h]
        ss = []
        for i in range(len(kparts)):
            s = raw[h][i] + biases[i][h]
            if valid is not None and valid[i] is not None:
                s = jnp.where(valid[i], s, NEG)
            ss.append(s)
        m = functools.reduce(jnp.maximum, [jnp.max(s, axis=-1, keepdims=True) for s in ss])
        ps = [jnp.exp(s - m) for s in ss]
        den = functools.reduce(jnp.add, [jnp.sum(p, axis=-1, keepdims=True) for p in ps])
        o = functools.reduce(jnp.add, [jnp.dot(p.astype(jnp.bfloat16), v, preferred_element_type=jnp.float32)
                                       for p, v in zip(ps, vparts)])
        out = out + jnp.where(hm, o / den, 0.0)
    return out


def _band_prompt_kernel(q_ref, k_ref, v_ref, bias_ref, o_ref):
    i = pl.program_id(1)
    T = TQ_BAND
    nprev = BAND // T
    starts = [pl.multiple_of(jnp.maximum(i - (nprev - j), 0) * T, T) for j in range(nprev + 1)]
    k = jnp.concatenate([k_ref[pl.ds(s, T), :] for s in starts], axis=0)
    v = jnp.concatenate([v_ref[pl.ds(s, T), :] for s in starts], axis=0)
    col = lax.broadcasted_iota(jnp.int32, (1, BAND + T), 1)
    valid = col >= BAND - i * T
    o_ref[...] = _band_heads(q_ref[...], [k], [v], [bias_ref], [valid])


def _band_prompt(q, k, v, bias):
    B, S, W = q.shape
    T = TQ_BAND
    assert S % T == 0 and BAND % T == 0
    return pl.pallas_call(
        _band_prompt_kernel,
        grid=(B, S // T),
        in_specs=[pl.BlockSpec((None, T, W), lambda b, i: (b, i, 0)),
                  pl.BlockSpec((None, S, W), lambda b, i: (b, 0, 0)),
                  pl.BlockSpec((None, S, W), lambda b, i: (b, 0, 0)),
                  pl.BlockSpec(bias.shape, lambda b, i: (0, 0, 0))],
        out_specs=pl.BlockSpec((None, T, W), lambda b, i: (b, i, 0)),
        out_shape=jax.ShapeDtypeStruct((B, S, W), jnp.float32),
        compiler_params=_params(("arbitrary", "arbitrary")),
        name="band_prompt",
    )(q, k, v, bias)


def _band_sample_kernel(q_ref, kn_ref, vn_ref, ck_ref, cv_ref, bc_ref, bn_ref, o_ref, sk_ref, sv_ref):
    bf = jnp.bfloat16
    ck, cv, kn, vn = ck_ref[...], cv_ref[...], kn_ref[...], vn_ref[...]
    o_ref[...] = _band_heads(q_ref[...], [ck.astype(bf), kn.astype(bf)], [cv.astype(bf), vn.astype(bf)],
                             [bc_ref, bn_ref], None)
    n_keep = ck.shape[0] - kn.shape[0]
    sk_ref[0:n_keep, :] = ck[kn.shape[0]:, :]
    sk_ref[n_keep:, :] = kn
    sv_ref[0:n_keep, :] = cv[vn.shape[0]:, :]
    sv_ref[n_keep:, :] = vn


def _band_sample(q, k_new, v_new, cache_k, cache_v, layer, bias_c, bias_n):
    B, T, W = q.shape
    LA = cache_k.shape[2]
    new = pl.BlockSpec((None, T, W), lambda b: (b, 0, 0))
    cache = pl.BlockSpec((None, None, LA, W), lambda b: (layer, b, 0, 0))
    roll = pl.BlockSpec((None, LA, W), lambda b: (b, 0, 0))
    full = lambda a: pl.BlockSpec(a.shape, lambda b: (0,) * a.ndim)
    return pl.pallas_call(
        _band_sample_kernel,
        grid=(B,),
        in_specs=[new, new, new, cache, cache, full(bias_c), full(bias_n)],
        out_specs=[new, roll, roll],
        out_shape=[jax.ShapeDtypeStruct((B, T, W), jnp.float32),
                   jax.ShapeDtypeStruct((B, LA, W), jnp.float32),
                   jax.ShapeDtypeStruct((B, LA, W), jnp.float32)],
        compiler_params=_params(("arbitrary",)),
        name="band_sample",
    )(q, k_new, v_new, cache_k, cache_v, bias_c, bias_n)


def _mla_prompt_kernel(qt_ref, k_ref, vt_ref, o_ref, m_ref, l_ref, acc_ref, sa_ref, sb_ref):
    i = pl.program_id(2)
    TQ = TK = T_MLA
    m_ref[...] = jnp.full(m_ref.shape, NEG, jnp.float32)
    l_ref[...] = jnp.zeros(l_ref.shape, jnp.float32)
    acc_ref[...] = jnp.zeros(acc_ref.shape, jnp.float32)

    def scores(hh, j):
        start = pl.multiple_of(j * TK, TK)
        return jnp.dot(k_ref[hh, pl.ds(start, TK), :], qt_ref[hh], preferred_element_type=jnp.float32)

    def step(j, src_ref, dst_ref, mask):
        for hh in range(HP_MLA):
            st = src_ref[hh]
            if mask is not None:
                st = jnp.where(mask, st, NEG)
            m_old = m_ref[hh]
            m_new = jnp.maximum(m_old, jnp.max(st, axis=0, keepdims=True))
            alpha = jnp.exp2(m_old - m_new)
            pt = jnp.exp2(st - m_new)
            l_ref[hh] = alpha * l_ref[hh] + jnp.sum(pt, axis=0, keepdims=True)
            if dst_ref is not None:
                dst_ref[hh] = scores(hh, j + 1)
            vt = vt_ref[j, hh * MLA_V:(hh + 1) * MLA_V, :]
            acc_ref[hh] = alpha * acc_ref[hh] + jnp.dot(vt, pt.astype(jnp.bfloat16),
                                                        preferred_element_type=jnp.float32)
            m_ref[hh] = m_new

    def pair(n, c):
        step(2 * n, sa_ref, sb_ref, None)
        step(2 * n + 1, sb_ref, sa_ref, None)
        return c

    r = lax.broadcasted_iota(jnp.int32, (TK, TQ), 0) // CHUNK
    c = lax.broadcasted_iota(jnp.int32, (TK, TQ), 1) // CHUNK
    diag = r <= c
    for hh in range(HP_MLA):
        sa_ref[hh] = scores(hh, 0)
    lax.fori_loop(0, i // 2, pair, 0)

    @pl.when(i % 2 == 0)
    def _():
        step(i, sa_ref, None, diag)

    @pl.when(i % 2 == 1)
    def _():
        step(i - 1, sa_ref, sb_ref, None)
        step(i, sb_ref, None, diag)

    for hh in range(HP_MLA):
        o_ref[:, hh * MLA_V:(hh + 1) * MLA_V] = (acc_ref[hh] / l_ref[hh]).T


def _mla_prompt(qmt, km, vmt):
    H, B, S, E = km.shape
    TQ = TK = T_MLA
    HP = HP_MLA
    assert S % TQ == 0 and H % HP == 0 and vmt.shape == (B, S // TK, H * MLA_V, TK)
    assert qmt.shape == (H, B, S // TQ, E, TQ)
    return pl.pallas_call(
        _mla_prompt_kernel,
        grid=(B, H // HP, S // TQ),
        in_specs=[pl.BlockSpec((HP, None, None, E, TQ), lambda b, h, i: (h, b, i, 0, 0)),
                  pl.BlockSpec((HP, None, S, E), lambda b, h, i: (h, b, 0, 0)),
                  pl.BlockSpec((None, S // TK, HP * MLA_V, TK), lambda b, h, i: (b, 0, h, 0))],
        out_specs=pl.BlockSpec((None, TQ, HP * MLA_V), lambda b, h, i: (b, i, h)),
        out_shape=jax.ShapeDtypeStruct((B, S, H * MLA_V), jnp.float32),
        scratch_shapes=[pltpu.VMEM((HP, 1, TQ), jnp.float32), pltpu.VMEM((HP, 1, TQ), jnp.float32),
                        pltpu.VMEM((HP, MLA_V, TQ), jnp.float32),
                        pltpu.VMEM((HP, TK, TQ), jnp.float32), pltpu.VMEM((HP, TK, TQ), jnp.float32)],
        compiler_params=_params(("arbitrary", "arbitrary", "arbitrary")),
        name="mla_prompt",
    )(qmt, km, vmt)


def _mla_sample_kernel(q_ref, kn_ref, vn_ref, ckv_ref, ckr_ref, wukv_ref, o_ref, kv_ref):
    bf = jnp.bfloat16
    P = ckv_ref.shape[0]
    step = min(512, P)
    for r in range(0, P, step):
        kv_ref[r:r + step, :] = jnp.dot(ckv_ref[r:r + step, :].astype(bf), wukv_ref[...],
                                        preferred_element_type=jnp.float32).astype(bf)
    ckr = ckr_ref[...].astype(bf)
    nk = N_HEADS * MLA_NOPE
    for h in range(N_HEADS):
        q = q_ref[h]
        s_c = (lax.dot_general(q[:, :MLA_NOPE], kv_ref[:, h * MLA_NOPE:(h + 1) * MLA_NOPE], _DN_T,
                               preferred_element_type=jnp.float32)
               + lax.dot_general(q[:, MLA_NOPE:MLA_NOPE + MLA_ROPE], ckr, _DN_T,
                                 preferred_element_type=jnp.float32))
        s_n = lax.dot_general(q, kn_ref[h], _DN_T, preferred_element_type=jnp.float32)
        m = jnp.maximum(jnp.max(s_c, axis=-1, keepdims=True), jnp.max(s_n, axis=-1, keepdims=True))
        p_c = jnp.exp2(s_c - m)
        p_n = jnp.exp2(s_n - m)
        den = jnp.sum(p_c, axis=-1, keepdims=True) + jnp.sum(p_n, axis=-1, keepdims=True)
        o = (jnp.dot(p_c.astype(bf), kv_ref[:, nk + h * MLA_V:nk + (h + 1) * MLA_V],
                     preferred_element_type=jnp.float32)
             + jnp.dot(p_n.astype(bf), vn_ref[:, h * MLA_V:(h + 1) * MLA_V],
                       preferred_element_type=jnp.float32))
        o_ref[:, h * MLA_V:(h + 1) * MLA_V] = o / den


def _mla_sample(qm, km, vm, cache_ckv, cache_kr, layer, w_ukv):
    H, B, T, E = qm.shape
    P = cache_ckv.shape[2]
    heads = pl.BlockSpec((H, None, T, E), lambda b: (0, b, 0, 0))
    return pl.pallas_call(
        _mla_sample_kernel,
        grid=(B,),
        in_specs=[heads, heads,
                  pl.BlockSpec((None, T, H * MLA_V), lambda b: (b, 0, 0)),
                  pl.BlockSpec((None, None, P, MLA_RANK), lambda b: (layer, b, 0, 0)),
                  pl.BlockSpec((None, None, P, MLA_ROPE), lambda b: (layer, b, 0, 0)),
                  pl.BlockSpec(w_ukv.shape, lambda b: (0, 0))],
        out_specs=pl.BlockSpec((None, T, H * MLA_V), lambda b: (b, 0, 0)),
        out_shape=jax.ShapeDtypeStruct((B, T, H * MLA_V), jnp.float32),
        scratch_shapes=[pltpu.VMEM((P, w_ukv.shape[1]), jnp.bfloat16)],
        compiler_params=_params(("arbitrary",)),
        name="mla_sample",
    )(qm, km, vm, cache_ckv, cache_kr, w_ukv)


def _neg_tri(n):
    r = lax.broadcasted_iota(jnp.int32, (n, n), 0)
    c = lax.broadcasted_iota(jnp.int32, (n, n), 1)
    return jnp.where(r >= c, -1.0, 0.0).astype(jnp.bfloat16)


def _sb_init(q, qh_ref, acc_ref, car_ref):
    for h in range(N_HEADS):
        qh_ref[h] = jnp.where(_head_mask(HEADS_W, h), q, jnp.zeros_like(q))
    acc_ref[...] = jnp.zeros(acc_ref.shape, jnp.float32)
    car_ref[...] = jnp.zeros(car_ref.shape, jnp.float32)


def _sb_tile(qh_ref, k, v, acc_ref, car_ref, ntri, mask, z_ref=None, next_k=None, zn_ref=None):
    def scores(h, keys=k):
        return lax.dot_general(qh_ref[h], keys, _DN_T, preferred_element_type=jnp.float32)

    def cumsum(z):
        neg_abs = lax.bitcast_convert_type(lax.bitcast_convert_type(z, jnp.uint32) | jnp.uint32(0x80000000),
                                           jnp.float32)
        sp = jnp.maximum(z, 0.0) + jnp.log2(1.0 + jnp.exp2(neg_abs))
        if mask is not None:
            sp = jnp.where(mask, sp, 0.0)
        return jnp.dot(sp.astype(jnp.bfloat16), ntri, preferred_element_type=jnp.float32)

    def accumulate(h, z, inner):
        a = jnp.minimum(jnp.exp2(z + inner + car_ref[h]), 1.0)
        if mask is not None:
            a = jnp.where(mask, a, 0.0)
        acc_ref[h] += jnp.dot(a.astype(jnp.bfloat16), v, preferred_element_type=jnp.float32)
        car_ref[h] += inner[:, 0:1]

    zs, inners = {}, {}
    if z_ref is not None:
        for t in range(N_HEADS + 1):
            if t < N_HEADS:
                if next_k is not None:
                    zn_ref[t] = scores(t, next_k)
                inners[t] = cumsum(z_ref[t])
            if t >= 1:
                accumulate(t - 1, z_ref[t - 1], inners.pop(t - 1))
        return
    zs[0] = scores(0)
    for t in range(1, N_HEADS + 2):
        if t < N_HEADS:
            zs[t] = scores(t)
        if 0 <= t - 1 < N_HEADS:
            inners[t - 1] = cumsum(zs[t - 1])
        if 0 <= t - 2 < N_HEADS:
            accumulate(t - 2, zs.pop(t - 2), inners.pop(t - 2))


def _sb_finish(acc_ref):
    return functools.reduce(jnp.add, [jnp.where(_head_mask(HEADS_W, h), acc_ref[h], 0.0)
                                      for h in range(N_HEADS)])


def _sb_prompt_kernel(q_ref, k_ref, v_ref, o_ref, qh_ref, acc_ref, car_ref, za_ref, zb_ref):
    i = pl.program_id(1)
    TQ, TK = TQ_SB, TK_SB
    _sb_init(q_ref[...], qh_ref, acc_ref, car_ref)
    ntri = _neg_tri(TK)
    row = i * TQ + lax.broadcasted_iota(jnp.int32, (TQ, TK), 0)
    col = lax.broadcasted_iota(jnp.int32, (TQ, TK), 1)

    def keys(t):
        return k_ref[pl.ds(pl.multiple_of(t * TK, TK), TK), :]

    def step(t, src_ref, dst_ref, masked):
        mask = (t * TK + col < row) if masked else None
        _sb_tile(qh_ref, None, v_ref[pl.ds(pl.multiple_of(t * TK, TK), TK), :], acc_ref, car_ref, ntri, mask,
                 z_ref=src_ref, next_k=keys(jnp.maximum(t - 1, 0)), zn_ref=dst_ref)

    first = 2 * i + 1
    for h in range(N_HEADS):
        za_ref[h] = lax.dot_general(qh_ref[h], keys(first), _DN_T, preferred_element_type=jnp.float32)
    step(first, za_ref, zb_ref, True)
    step(first - 1, zb_ref, za_ref, True)

    def pair(n, c):
        t = first - 2 - 2 * n
        step(t, za_ref, zb_ref, False)
        step(t - 1, zb_ref, za_ref, False)
        return c

    lax.fori_loop(0, i, pair, 0)
    o_ref[...] = _sb_finish(acc_ref)


def _sb_prompt(q, k, v):
    B, S, W = q.shape
    TQ, TK = TQ_SB, TK_SB
    assert S % TQ == 0 and TQ == 2 * TK
    return pl.pallas_call(
        _sb_prompt_kernel,
        grid=(B, S // TQ),
        in_specs=[pl.BlockSpec((None, TQ, W), lambda b, i: (b, i, 0)),
                  pl.BlockSpec((None, S, W), lambda b, i: (b, 0, 0)),
                  pl.BlockSpec((None, S, W), lambda b, i: (b, 0, 0))],
        out_specs=pl.BlockSpec((None, TQ, W), lambda b, i: (b, i, 0)),
        out_shape=jax.ShapeDtypeStruct((B, S, W), jnp.float32),
        scratch_shapes=[pltpu.VMEM((N_HEADS, TQ, W), jnp.bfloat16), pltpu.VMEM((N_HEADS, TQ, W), jnp.float32),
                        pltpu.VMEM((N_HEADS, TQ, 1), jnp.float32),
                        pltpu.VMEM((N_HEADS, TQ, TK), jnp.float32), pltpu.VMEM((N_HEADS, TQ, TK), jnp.float32)],
        compiler_params=_params(("arbitrary", "arbitrary")),
        name="sb_prompt",
    )(q, k, v)


def _sb_sample_kernel(q_ref, kn_ref, vn_ref, ck_ref, cv_ref, o_ref, kb_ref, vb_ref, qh_ref, acc_ref, car_ref):
    bf = jnp.bfloat16
    T = q_ref.shape[0]
    P = ck_ref.shape[0]
    TK = min(TK_SB, P)
    kb_ref[...] = ck_ref[...].astype(bf)
    vb_ref[...] = cv_ref[...].astype(bf)
    _sb_init(q_ref[...], qh_ref, acc_ref, car_ref)
    r = lax.broadcasted_iota(jnp.int32, (T, T), 0)
    c = lax.broadcasted_iota(jnp.int32, (T, T), 1)
    _sb_tile(qh_ref, kn_ref[...], vn_ref[...], acc_ref, car_ref, _neg_tri(T), c < r)
    ntri = _neg_tri(TK)

    def body(n, carry):
        start = pl.multiple_of(P - (n + 1) * TK, TK)
        _sb_tile(qh_ref, kb_ref[pl.ds(start, TK), :], vb_ref[pl.ds(start, TK), :], acc_ref, car_ref, ntri, None)
        return carry

    lax.fori_loop(0, P // TK, body, 0)
    o_ref[...] = _sb_finish(acc_ref)


def _sb_sample(q, k_new, v_new, cache_k, cache_v, layer):
    B, T, W = q.shape
    P = cache_k.shape[2]
    assert P % min(TK_SB, P) == 0
    new = pl.BlockSpec((None, T, W), lambda b: (b, 0, 0))
    cache = pl.BlockSpec((None, None, P, W), lambda b: (layer, b, 0, 0))
    return pl.pallas_call(
        _sb_sample_kernel,
        grid=(B,),
        in_specs=[new, new, new, cache, cache],
        out_specs=new,
        out_shape=jax.ShapeDtypeStruct((B, T, W), jnp.float32),
        scratch_shapes=[pltpu.VMEM((P, W), jnp.bfloat16), pltpu.VMEM((P, W), jnp.bfloat16),
                        pltpu.VMEM((N_HEADS, T, W), jnp.bfloat16), pltpu.VMEM((N_HEADS, T, W), jnp.float32),
                        pltpu.VMEM((N_HEADS, T, 1), jnp.float32)],
        compiler_params=_params(("arbitrary",)),
        name="sb_sample",
    )(q, k_new, v_new, cache_k, cache_v)


def _merge_ffn_kernel(final, x_ref, oa_ref, om_ref, os_ref, goa_ref, gom_ref, gos_ref, wout_ref,
                      gffn_ref, wup_ref, wdown_ref, gfin_ref, y_ref, x1_ref, h_ref, acc_ref):
    bf = jnp.bfloat16
    j = pl.program_id(1)

    @pl.when(j == 0)
    def _():
        cat = jnp.concatenate([_rms(oa_ref[...], goa_ref[...]).astype(bf),
                               _rms(om_ref[...], gom_ref[...]).astype(bf),
                               _rms(os_ref[...], gos_ref[...]).astype(bf)], axis=-1)
        x1 = x_ref[...] + jnp.dot(cat, wout_ref[...], preferred_element_type=jnp.float32)
        x1_ref[...] = x1
        h_ref[...] = _rms(x1, gffn_ref[...]).astype(bf)
        acc_ref[...] = jnp.zeros(acc_ref.shape, jnp.float32)

    u = jnp.maximum(jnp.dot(h_ref[...], wup_ref[...], preferred_element_type=jnp.float32), 0.0)
    acc_ref[...] += jnp.dot((u * u).astype(bf), wdown_ref[...], preferred_element_type=jnp.float32)

    @pl.when(j == pl.num_programs(1) - 1)
    def _():
        y = x1_ref[...] + acc_ref[...]
        y_ref[...] = _rms(y, gfin_ref[...]) if final else y


def _merge_ffn(x, oa, om, osb, g_oa, g_om, g_os, w_out, g_ffn, w_up, w_down, g_final, final):
    N, D = x.shape
    F = w_up.shape[1]
    tm = min(TM_FFN, N)
    tf = min(TF_FFN, F)
    assert N % tm == 0 and F % tf == 0
    row = lambda w: pl.BlockSpec((tm, w), lambda i, j: (i, 0))
    full = lambda a: pl.BlockSpec(a.shape, lambda i, j: (0,) * a.ndim)
    return pl.pallas_call(
        functools.partial(_merge_ffn_kernel, final),
        grid=(N // tm, F // tf),
        in_specs=[row(D), row(oa.shape[1]), row(om.shape[1]), row(osb.shape[1]),
                  full(g_oa), full(g_om), full(g_os), full(w_out), full(g_ffn),
                  pl.BlockSpec((D, tf), lambda i, j: (0, j)),
                  pl.BlockSpec((tf, D), lambda i, j: (j, 0)),
                  full(g_final)],
        out_specs=row(D),
        out_shape=jax.ShapeDtypeStruct((N, D), jnp.float32),
        scratch_shapes=[pltpu.VMEM((tm, D), jnp.float32), pltpu.VMEM((tm, D), jnp.bfloat16),
                        pltpu.VMEM((tm, D), jnp.float32)],
        compiler_params=_params(("arbitrary", "arbitrary")),
        name="merge_ffn",
    )(x, oa, om, osb, g_oa, g_om, g_os, w_out, g_ffn, w_up, w_down, g_final)


def _swap_halves(w):
    half = w.shape[-1] // 2
    return jnp.concatenate([w[..., half:], w[..., :half]], axis=-1)


def _pad_cols(w, width):
    return jnp.pad(w, [(0, 0)] * (w.ndim - 1) + [(0, width - w.shape[-1])])


def _prep_layer(w_in, w_uq, w_ukv):
    bf = jnp.bfloat16
    W = HEADS_W
    a_end = 3 * W
    cq_end = a_end + MLA_RANK
    ckv_end = cq_end + MLA_RANK
    kr_end = ckv_end + MLA_ROPE
    w_in = w_in.astype(bf)
    w_kr = w_in[:, ckv_end:kr_end]
    n_main = w_in.shape[1] - MLA_ROPE
    w_in_p = jnp.zeros((w_in.shape[0], n_main + 2 * ROPE_PAD), bf)
    w_in_p = w_in_p.at[:, :ckv_end].set(w_in[:, :ckv_end])
    w_in_p = w_in_p.at[:, ckv_end:n_main].set(w_in[:, kr_end:])
    w_in_p = w_in_p.at[:, n_main:n_main + MLA_ROPE].set(w_kr)
    w_in_p = w_in_p.at[:, n_main + ROPE_PAD:n_main + ROPE_PAD + MLA_ROPE].set(_swap_halves(w_kr))
    qn = w_uq[:, :, :MLA_NOPE].reshape(MLA_RANK, N_HEADS * MLA_NOPE)
    qp = w_uq[:, :, MLA_NOPE:]
    w_uq_p = jnp.concatenate([qn, _pad_cols(qp, ROPE_PAD).reshape(MLA_RANK, -1),
                              _pad_cols(_swap_halves(qp), ROPE_PAD).reshape(MLA_RANK, -1)], axis=1)
    w_ukv_p = jnp.concatenate([w_ukv[:, :, :MLA_NOPE].reshape(MLA_RANK, -1),
                               w_ukv[:, :, MLA_NOPE:].reshape(MLA_RANK, -1)], axis=1)
    return w_in_p.astype(bf), w_uq_p.astype(bf), w_ukv_p.astype(bf)


def _rope_tables(pos):
    half = MLA_ROPE // 2
    inv = ROPE_THETA ** (-jnp.arange(half, dtype=jnp.float32) / half)
    ang = pos.astype(jnp.float32)[:, None] * inv[None, :]
    cos, sin = jnp.cos(ang), jnp.sin(ang)
    return (_pad_cols(jnp.concatenate([cos, cos], axis=-1), ROPE_PAD),
            _pad_cols(jnp.concatenate([-sin, sin], axis=-1), ROPE_PAD))


def _band_bias_chunk(rel):
    R = rel.shape[-1]
    assert R == CHUNK + REL_MAX
    K = BAND + CHUNK
    L = K + CHUNK - 1
    lead = rel.shape[:-1]
    g = jnp.concatenate([rel, jnp.broadcast_to(rel[..., R - 1:], lead + (L - R,))], axis=-1)
    g = _pad_cols(jnp.flip(g, axis=-1), L + 1)
    m = jnp.tile(g, (1,) * len(lead) + (CHUNK,))[..., :CHUNK * L].reshape(lead + (CHUNK, L))
    return m[..., CHUNK - 1:CHUNK - 1 + K]


def _band_bias_prompt(chunk_bias):
    K = chunk_bias.shape[-1]
    out = jnp.full(chunk_bias.shape[:-2] + (TQ_BAND, BAND + TQ_BAND), NEG, jnp.float32)
    for c in range(TQ_BAND // CHUNK):
        out = out.at[..., CHUNK * c:CHUNK * (c + 1), CHUNK * c:CHUNK * c + K].set(chunk_bias)
    return out


def kernel(x_prompt, x_sample, cache_a_k, cache_a_v, cache_mla_ckv, cache_mla_krope, cache_sb_k, cache_sb_v,
           g_mix, w_in, g_cq, g_ckv, w_uq, w_ukv, a_rel_bias, g_out_a, g_out_mla, g_out_sb, w_out,
           g_ffn, w_up, w_down, g_final):
    bf = jnp.bfloat16
    B, S, D = x_prompt.shape
    DB, T, _ = x_sample.shape
    depth = w_in.shape[0]
    LA = cache_a_k.shape[2]
    P = cache_mla_ckv.shape[2]
    lc = min(BAND, S)
    assert T == CHUNK and P % CHUNK == 0 and LA == BAND and lc == BAND

    cache_a_k = cache_a_k.reshape(depth, DB, LA, HEADS_W)
    cache_a_v = cache_a_v.reshape(depth, DB, LA, HEADS_W)
    cache_sb_k = cache_sb_k.reshape(depth, DB, P, HEADS_W)
    cache_sb_v = cache_sb_v.reshape(depth, DB, P, HEADS_W)

    tm_s = min(TM_IN, DB * T)
    cos_p, sin_p = _rope_tables(jnp.arange(S))
    cos_s, sin_s = _rope_tables(P + jnp.arange(tm_s) % T)
    row = lambda g: g.reshape(1, -1)
    chunk_bias = _band_bias_chunk(a_rel_bias)
    bias_p = _band_bias_prompt(chunk_bias)

    xp = x_prompt.reshape(B * S, D)
    xs = x_sample.reshape(DB * T, D)
    p_states, s_states = [], []
    for l in range(depth):
        w_in_p, w_uq_p, w_ukv_p = _prep_layer(w_in[l], w_uq[l], w_ukv[l])
        w_out_b, w_up_b, w_down_b = w_out[l].astype(bf), w_up[l].astype(bf), w_down[l].astype(bf)
        last = l == depth - 1
        lw_in = (row(g_mix[l]), w_in_p, row(g_cq[l]), row(g_ckv[l]), w_uq_p, w_ukv_p)
        lw_out = (row(g_out_a[l]), row(g_out_mla[l]), row(g_out_sb[l]), w_out_b, row(g_ffn[l]),
                  w_up_b, w_down_b, row(g_final), last)

        (qa, ka, va, kaf, vaf, ckv, kr, qc, kc, vc, kcf, vcf, _, km, _, vmt, qmt) = _in_proj(
            xp, *lw_in, cos_p, sin_p, S, lc)
        seq = lambda a: a.reshape(B, S, a.shape[-1])
        oa = _band_prompt(seq(qa), seq(ka), seq(va), bias_p[l])
        om = _mla_prompt(qmt.reshape(N_HEADS, B, S // T_MLA, MLA_QK, T_MLA), km.reshape(N_HEADS, B, S, MLA_QK),
                         vmt.reshape(B, S // T_MLA, N_HEADS * MLA_V, T_MLA))
        osb = _sb_prompt(seq(qc), seq(kc), seq(vc))
        xp = _merge_ffn(xp, oa.reshape(B * S, -1), om.reshape(B * S, -1), osb.reshape(B * S, -1), *lw_out)
        p_states.append((kaf.reshape(B, lc, N_HEADS, HEAD_DIM), vaf.reshape(B, lc, N_HEADS, HEAD_DIM),
                         ckv.reshape(B, S, MLA_RANK), kr.reshape(B, S, MLA_ROPE),
                         kcf.reshape(B, S, N_HEADS, HEAD_DIM), vcf.reshape(B, S, N_HEADS, HEAD_DIM)))

        (qa, ka, va, kaf, vaf, ckv, kr, qc, kc, vc, kcf, vcf, qm, km, vm, _, _) = _in_proj(
            xs, *lw_in, cos_s, sin_s, T, T)
        seq = lambda a: a.reshape(DB, T, a.shape[-1])
        bias_c, bias_n = chunk_bias[l, :, :, :LA], chunk_bias[l, :, :, LA:]
        oa, sk, sv = _band_sample(seq(qa), seq(kaf), seq(vaf), cache_a_k, cache_a_v, l, bias_c, bias_n)
        om = _mla_sample(qm.reshape(N_HEADS, DB, T, MLA_QK), km.reshape(N_HEADS, DB, T, MLA_QK), seq(vm),
                         cache_mla_ckv, cache_mla_krope, l, w_ukv_p)
        osb = _sb_sample(seq(qc), seq(kc), seq(vc), cache_sb_k, cache_sb_v, l)
        xs = _merge_ffn(xs, oa.reshape(DB * T, -1), om.reshape(DB * T, -1), osb.reshape(DB * T, -1), *lw_out)
        s_states.append((sk.reshape(DB, LA, N_HEADS, HEAD_DIM), sv.reshape(DB, LA, N_HEADS, HEAD_DIM),
                         ckv.reshape(DB, T, MLA_RANK), kr.reshape(DB, T, MLA_ROPE),
                         kcf.reshape(DB, T, N_HEADS, HEAD_DIM), vcf.reshape(DB, T, N_HEADS, HEAD_DIM)))

    p_out = [jnp.stack(t, axis=0) for t in zip(*p_states)]
    s_out = [jnp.stack(t, axis=0) for t in zip(*s_states)]
    return (xp.reshape(B, S, D), xs.reshape(DB, T, D), *p_out, *s_out)
```
